```python
import math
import jax, jax.numpy as jnp
from jax import lax
import numpy as np

D_MODEL = 2048
BATCH = 1
SEQ = 16384
DEPTH = 4

HEAD_DIM = 64
RWKV_WIDTH = D_MODEL // 2
RWKV_HEADS = RWKV_WIDTH // HEAD_DIM
LORA_W = max(32, int(round(1.8 * RWKV_WIDTH ** 0.5 / 32)) * 32)
LORA_A = max(32, int(round(1.8 * RWKV_WIDTH ** 0.5 / 32)) * 32)
LORA_V = max(32, int(round(1.3 * RWKV_WIDTH ** 0.5 / 32)) * 32)
LORA_G = max(32, int(round(0.6 * RWKV_WIDTH ** 0.8 / 32)) * 32)
RWKV_IN = 3 * RWKV_WIDTH + LORA_W + LORA_A + LORA_G
S5_WIDTH = D_MODEL - RWKV_WIDTH
S5_GROUP = 16
S5_GROUPS = S5_WIDTH // S5_GROUP
S5_STATE = 64
EVEN_IN = RWKV_IN + S5_WIDTH
EVEN_MIX = RWKV_WIDTH + S5_WIDTH
ATTN_HEADS = D_MODEL // HEAD_DIM
ROT_DIM = HEAD_DIM // 4
ROPE_THETA = 500000.0
DILATED_BRANCHES = ((128, 1), (512, 4), (2048, 16))
SEQ_PAD_MULT = math.lcm(*[w for w, _ in DILATED_BRANCHES])
FFN_HIDDEN = -(-8 * D_MODEL // (3 * 256)) * 256
N_EVEN = (DEPTH + 1) // 2
N_ODD = DEPTH // 2
RMS_EPS = 1e-6
GN_EPS = 64e-5

kernel_name = 'hybrid_rwkv7_s5_dilated_attn_trunk'


def rms_norm(x, g):
    x32 = x.astype(jnp.float32)
    return x32 * lax.rsqrt(jnp.mean(x32 * x32, -1, keepdims=True) + RMS_EPS) * g.astype(jnp.float32)


def modulate(h, shift, scale):
    return h * (1.0 + scale[:, None, :]) + shift[:, None, :]


def token_shift(p):
    return jnp.pad(p, ((0, 0), (1, 0), (0, 0)))[:, :-1]


def split_cols(p, sizes):
    return jnp.split(p, np.cumsum(sizes)[:-1].tolist(), axis=-1)


def rwkv7_recurrence(r, decay, k, v, a, b):
    def step(state, inp):
        r_t, w_t, k_t, v_t, a_t, b_t = inp
        sa = jnp.einsum('bhvk,bhk->bhv', state, a_t)
        state = (state * w_t[:, :, None, :] + sa[..., None] * b_t[:, :, None, :]
                 + v_t[..., None] * k_t[:, :, None, :])
        return state, jnp.einsum('bhvk,bhk->bhv', state, r_t)
    B, S, H, N = r.shape
    s0 = jnp.zeros((B, H, N, N), jnp.float32)
    _, y = lax.scan(step, s0, tuple(jnp.moveaxis(t, 1, 0) for t in (r, decay, k, v, a, b)))
    return jnp.moveaxis(y, 0, 1)


def rwkv7_time_mix(p, mu, w0, w2, a0, a2, g2, k_k, k_a, r_k, ln_w, ln_b, v_first, vres):
    B, S, _ = p.shape
    H, N, C = RWKV_HEADS, HEAD_DIM, RWKV_WIDTH
    p = p.astype(jnp.float32)
    p = p + (token_shift(p) - p) * mu
    r, k, v, wl, al, gl = split_cols(p, (C, C, C, LORA_W, LORA_A, LORA_G))
    w = -jax.nn.softplus(-(w0 + jnp.tanh(wl) @ w2)) - 0.5
    a = jax.nn.sigmoid(a0 + al @ a2)
    g = jax.nn.sigmoid(gl) @ g2
    if vres is not None:
        v0, v1, v2 = vres
        v = v + (v_first - v) * jax.nn.sigmoid(v0 + (v @ v1) @ v2)
    heads = lambda t: t.reshape(B, S, H, N)
    kk = heads(k * k_k)
    kk = kk / jnp.maximum(jnp.sqrt(jnp.sum(kk * kk, -1, keepdims=True)), 1e-12)
    k = k * (1.0 + (a - 1.0) * k_a)
    rh, kh, vh = heads(r), heads(k), heads(v)
    y = rwkv7_recurrence(rh, heads(jnp.exp(-jnp.exp(w))), kh, vh, -kk, kk * heads(a))
    mean = jnp.mean(y, -1, keepdims=True)
    var = jnp.mean(jnp.square(y - mean), -1, keepdims=True)
    y = ((y - mean) * lax.rsqrt(var + GN_EPS)).reshape(B, S, C) * ln_w + ln_b
    y = y + (jnp.sum(rh * kh * r_k, -1, keepdims=True) * vh).reshape(B, S, C)
    return y * g, v


def s5_ssm(u, lam_re, lam_im, log_step, b_re, b_im, c_re, c_im, d, glu_w, glu_b):
    B, S, _ = u.shape
    u = u.astype(jnp.float32)
    lam = lax.complex(jnp.minimum(lam_re.astype(jnp.float32), -1e-4), lam_im.astype(jnp.float32))
    step = jnp.exp(log_step.astype(jnp.float32))[:, None]
    lam_bar = jnp.exp(lam * step)
    b_bar = ((lam_bar - 1.0) / lam)[..., None] * lax.complex(
        b_re.astype(jnp.float32), b_im.astype(jnp.float32))
    c_mat = lax.complex(c_re.astype(jnp.float32), c_im.astype(jnp.float32))
    ug = u.reshape(B, S, S5_GROUPS, S5_GROUP)
    bu = jnp.einsum('bsgh,gph->bsgp', ug.astype(jnp.complex64), b_bar)

    def combine(e1, e2):
        a1, x1 = e1
        a2, x2 = e2
        return a2 * a1, a2 * x1 + x2

    _, xs = lax.associative_scan(combine, (jnp.broadcast_to(lam_bar, bu.shape), bu), axis=1)
    y = jnp.real(jnp.einsum('bsgp,ghp->bsgh', xs, c_mat)).reshape(B, S, S5_WIDTH) + d * u
    y = jax.nn.gelu(y)
    return y * jax.nn.sigmoid(y @ glu_w + glu_b)


def rotary_partial(t, pos):
    half = ROT_DIM // 2
    inv = ROPE_THETA ** (-jnp.arange(half, dtype=jnp.float32) * 2.0 / ROT_DIM)
    ang = pos[:, None] * inv[None, :]
    cos, sin = jnp.cos(ang)[None, :, None, :], jnp.sin(ang)[None, :, None, :]
    t32 = t.astype(jnp.float32)
    x1, x2, rest = t32[..., :half], t32[..., half:ROT_DIM], t32[..., ROT_DIM:]
    return jnp.concatenate([x1 * cos - x2 * sin, x2 * cos + x1 * sin, rest], -1).astype(t.dtype)


def dilated_branch(q, k, v, window, dilation):
    B, Sp, H, Dh = q.shape
    blk = window // dilation
    nb = Sp // window
    blocks = lambda t: t.reshape(B, nb, blk, dilation, H, Dh)
    qb, kb, vb = blocks(q), blocks(k), blocks(v)
    prev = lambda t: jnp.pad(t, ((0, 0), (1, 0), (0, 0), (0, 0), (0, 0), (0, 0)))[:, :-1]
    kc = jnp.concatenate([prev(kb), kb], axis=2)
    vc = jnp.concatenate([prev(vb), vb], axis=2)
    s = jnp.einsum('bnqrhd,bnkrhd->bnrhqk', qb, kc).astype(jnp.float32) * (Dh ** -0.5)
    qi = jnp.arange(blk)[:, None]
    kj = jnp.arange(2 * blk)[None, :]
    steps_back = blk + qi - kj
    band = (steps_back >= 0) & (steps_back <= blk)
    no_prev = (jnp.arange(nb) == 0)[:, None, None] & (kj < blk)[None]
    mask = band[None] & ~no_prev
    s = jnp.where(mask[None, :, None, None], s, -jnp.inf)
    m = jnp.max(s, -1, keepdims=True)
    pexp = jnp.exp(s - m)
    den = jnp.sum(pexp, -1, keepdims=True)
    o = jnp.einsum('bnrhqk,bnkrhd->bnrhqd', pexp, vc.astype(jnp.float32)) / den
    lse = (m + jnp.log(den))[..., 0]
    o = jnp.transpose(o, (0, 1, 4, 2, 3, 5)).reshape(B, Sp, H, Dh)
    lse = jnp.transpose(lse, (0, 1, 4, 2, 3)).reshape(B, Sp, H)
    return o, lse


def dilated_attention(h, w_qkv, w_o):
    B, S, _ = h.shape
    q, k, v = jnp.split(h @ w_qkv, 3, axis=-1)
    q, k, v = (t.reshape(B, S, ATTN_HEADS, HEAD_DIM) for t in (q, k, v))
    pos = jnp.arange(S, dtype=jnp.float32)
    q, k = rotary_partial(q, pos), rotary_partial(k, pos)
    sp = -(-S // SEQ_PAD_MULT) * SEQ_PAD_MULT
    pad = lambda t: jnp.pad(t, ((0, 0), (0, sp - S), (0, 0), (0, 0)))
    q, k, v = pad(q), pad(k), pad(v)
    outs, lses = zip(*[dilated_branch(q, k, v, w, d) for (w, d) in DILATED_BRANCHES])
    wts = jax.nn.softmax(jnp.stack(lses), axis=0)
    o = jnp.sum(wts[..., None] * jnp.stack(outs), axis=0)[:, :S]
    return o.reshape(B, S, D_MODEL).astype(h.dtype) @ w_o


def swiglu(h, w_in, w_out):
    gate, up = jnp.split(h @ w_in, 2, axis=-1)
    return (jax.nn.silu(gate) * up) @ w_out


def setup_inputs(seed: int = 0) -> dict:
    key = jax.random.key(seed)
    ks = iter(jax.random.split(key, 48))
    nrm = lambda shape, std: jax.random.normal(next(ks), shape, jnp.float32) * std
    uni = lambda shape, lo, hi: jax.random.uniform(next(ks), shape, jnp.float32, lo, hi)
    E, O, L, D = N_EVEN, N_ODD, DEPTH, D_MODEL
    C, G, P, GS = RWKV_WIDTH, S5_GROUPS, S5_STATE, S5_GROUP
    lam_im0 = jnp.broadcast_to(jnp.pi * jnp.arange(P, dtype=jnp.float32), (E, G, P))
    return {
        'x': nrm((BATCH, SEQ, D), 1.0),
        'c': nrm((BATCH, D), 1.0),
        'ada_w': nrm((L, D, 6 * D), D ** -0.5),
        'ada_b': nrm((L, 6 * D), 0.01),
        'norm_mix_g': 1.0 + nrm((L, D), 0.01),
        'norm_ffn_g': 1.0 + nrm((L, D), 0.01),
        'hyb_w_in': nrm((E, D, EVEN_IN), D ** -0.5),
        'hyb_w_out': nrm((E, EVEN_MIX, D), EVEN_MIX ** -0.5),
        'rwkv_mu': uni((E, RWKV_IN), 0.0, 1.0),
        'rwkv_w0': uni((E, C), -6.0, -1.0),
        'rwkv_w2': nrm((E, LORA_W, C), 0.1 * LORA_W ** -0.5),
        'rwkv_a0': nrm((E, C), 0.1),
        'rwkv_a2': nrm((E, LORA_A, C), 0.1 * LORA_A ** -0.5),
        'rwkv_g2': nrm((E, LORA_G, C), LORA_G ** -0.5),
        'rwkv_k_k': 0.85 + nrm((E, C), 0.02),
        'rwkv_k_a': 1.0 + nrm((E, C), 0.02),
        'rwkv_r_k': nrm((E, RWKV_HEADS, HEAD_DIM), 0.1),
        'rwkv_ln_w': 1.0 + nrm((E, C), 0.01),
        'rwkv_ln_b': nrm((E, C), 0.01),
        'rwkv_v0': 1.0 + nrm((E - 1, C), 0.1),
        'rwkv_v1': nrm((E - 1, C, LORA_V), 0.1 * C ** -0.5),
        'rwkv_v2': nrm((E - 1, LORA_V, C), 0.1 * LORA_V ** -0.5),
        's5_lam_re': -0.5 + nrm((E, G, P), 0.01),
        's5_lam_im': lam_im0 + nrm((E, G, P), 0.01),
        's5_log_step': uni((E, G), math.log(1e-3), math.log(1e-1)),
        's5_b_re': nrm((E, G, P, GS), (2.0 * GS) ** -0.5),
        's5_b_im': nrm((E, G, P, GS), (2.0 * GS) ** -0.5),
        's5_c_re': nrm((E, G, GS, P), (2.0 * P) ** -0.5),
        's5_c_im': nrm((E, G, GS, P), (2.0 * P) ** -0.5),
        's5_d': nrm((E, S5_WIDTH), 1.0),
        's5_glu_w': nrm((E, S5_WIDTH, S5_WIDTH), S5_WIDTH ** -0.5),
        's5_glu_b': nrm((E, S5_WIDTH), 0.01),
        'attn_w_qkv': nrm((O, D, 3 * D), D ** -0.5),
        'attn_w_o': nrm((O, D, D), D ** -0.5),
        'ffn_w_in': nrm((L, D, 2 * FFN_HIDDEN), D ** -0.5),
        'ffn_w_out': nrm((L, FFN_HIDDEN, D), FFN_HIDDEN ** -0.5),
        'final_norm_g': 1.0 + nrm((D,), 0.01),
    }


def reference(x, c, ada_w, ada_b, norm_mix_g, norm_ffn_g, hyb_w_in, hyb_w_out,
              rwkv_mu, rwkv_w0, rwkv_w2, rwkv_a0, rwkv_a2, rwkv_g2, rwkv_k_k, rwkv_k_a,
              rwkv_r_k, rwkv_ln_w, rwkv_ln_b, rwkv_v0, rwkv_v1, rwkv_v2,
              s5_lam_re, s5_lam_im, s5_log_step, s5_b_re, s5_b_im, s5_c_re, s5_c_im,
              s5_d, s5_glu_w, s5_glu_b, attn_w_qkv, attn_w_o, ffn_w_in, ffn_w_out,
              final_norm_g):
    v_first = None
    for i in range(DEPTH):
        mod = jax.nn.silu(c) @ ada_w[i] + ada_b[i]
        sh_mix, sc_mix, gt_mix, sh_ffn, sc_ffn, gt_ffn = jnp.split(mod, 6, axis=-1)
        h = modulate(rms_norm(x, norm_mix_g[i]), sh_mix, sc_mix).astype(x.dtype)
        if i % 2 == 0:
            j = i // 2
            p = h @ hyb_w_in[j]
            vres = None if j == 0 else (rwkv_v0[j - 1], rwkv_v1[j - 1], rwkv_v2[j - 1])
            y_rwkv, v = rwkv7_time_mix(p[..., :RWKV_IN], rwkv_mu[j], rwkv_w0[j], rwkv_w2[j],
                                       rwkv_a0[j], rwkv_a2[j], rwkv_g2[j], rwkv_k_k[j],
                                       rwkv_k_a[j], rwkv_r_k[j], rwkv_ln_w[j], rwkv_ln_b[j],
                                       v_first, vres)
            if j == 0:
                v_first = v
            y_s5 = s5_ssm(p[..., RWKV_IN:], s5_lam_re[j], s5_lam_im[j], s5_log_step[j],
                          s5_b_re[j], s5_b_im[j], s5_c_re[j], s5_c_im[j], s5_d[j],
                          s5_glu_w[j], s5_glu_b[j])
            y = jnp.concatenate([y_rwkv, y_s5], axis=-1).astype(x.dtype) @ hyb_w_out[j]
        else:
            y = dilated_attention(h, attn_w_qkv[i // 2], attn_w_o[i // 2])
        x = x + (gt_mix[:, None, :] * y).astype(x.dtype)
        h = modulate(rms_norm(x, norm_ffn_g[i]), sh_ffn, sc_ffn).astype(x.dtype)
        x = x + (gt_ffn[:, None, :] * swiglu(h, ffn_w_in[i], ffn_w_out[i])).astype(x.dtype)
    return rms_norm(x, final_norm_g).astype(x.dtype)
```

```python
import functools
import math

import jax
import jax.numpy as jnp
from jax import lax
from jax.experimental import pallas as pl
from jax.experimental.pallas import tpu as pltpu

F32 = jnp.float32
BF16 = jnp.bfloat16
HI = lax.Precision.HIGHEST

D_MODEL = 2048
DEPTH = 4
HEAD_DIM = 64
RWKV_WIDTH = 1024
LORA_W = 64
LORA_A = 64
LORA_V = 32
LORA_G = 160
RWKV_IN = 3 * RWKV_WIDTH + LORA_W + LORA_A + LORA_G
RWKV_IN_PAD = 3456
S5_WIDTH = 1024
S5_GROUP = 16
S5_STATE = 64
EVEN_IN_PAD = RWKV_IN_PAD + S5_WIDTH
ATTN_HEADS = 32
ROT_DIM = 16
ROPE_THETA = 500000.0
DILATED_BRANCHES = ((128, 1), (512, 4), (2048, 16))
ATTN_BLK = 128
FFN_HIDDEN = 5632
RMS_EPS = 1e-6
GN_EPS = 64e-5

LANES = 128
CHUNK = 64
VMEM_LIMIT = 48 * 1024 * 1024


def _cparams(sem):
    return pltpu.CompilerParams(dimension_semantics=sem, vmem_limit_bytes=VMEM_LIMIT)


def _sigmoid(x):
    return 1.0 / (1.0 + jnp.exp(-x))


def _dot(a, b, prec=None):
    return jnp.dot(a, b, preferred_element_type=F32, precision=prec)


def _dot_nt(a, b, prec=None):
    return lax.dot_general(a, b, (((1,), (1,)), ((), ())), preferred_element_type=F32, precision=prec)


def _dot_tn(a, b, prec=None):
    return lax.dot_general(a, b, (((0,), (0,)), ((), ())), preferred_element_type=F32, precision=prec)


def _norm_mod(x, g, shift, scale):
    ms = jnp.mean(x * x, axis=-1, keepdims=True)
    return (x * lax.rsqrt(ms + RMS_EPS) * g) * (1.0 + scale) + shift


def _ada_kernel(c_ref, w_ref, b_ref, o_ref):
    c = c_ref[...]
    s = c * _sigmoid(c)
    o_ref[0] = jnp.sum(s * w_ref[0], axis=0, keepdims=True) + b_ref[0]


def ada_modulation(c, ada_w, ada_b, tn=1024):
    L, D, N = ada_w.shape
    return pl.pallas_call(
        _ada_kernel,
        out_shape=jax.ShapeDtypeStruct((L, 1, N), F32),
        grid=(L, N // tn),
        in_specs=[pl.BlockSpec((D, 1), lambda l, j: (0, 0)),
                  pl.BlockSpec((1, D, tn), lambda l, j: (l, 0, j)),
                  pl.BlockSpec((1, 1, tn), lambda l, j: (l, 0, j))],
        out_specs=pl.BlockSpec((1, 1, tn), lambda l, j: (l, 0, j)),
        compiler_params=_cparams(("parallel", "parallel")),
        name="ada_modulation",
    )(c.reshape(D, 1), ada_w, ada_b.reshape(L, 1, N))


def _nm_matmul_kernel(x_ref, g_ref, sh_ref, sc_ref, w_ref, *rest, n_rot_blocks):
    if n_rot_blocks:
        cos_ref, sa_ref, sb_ref, o_ref, h_ref = rest
    else:
        o_ref, h_ref = rest
    j = pl.program_id(1)

    @pl.when(j == 0)
    def _():
        h_ref[...] = _norm_mod(x_ref[...], g_ref[...], sh_ref[...], sc_ref[...]).astype(BF16)

    acc = _dot(h_ref[...], w_ref[...])
    if not n_rot_blocks:
        o_ref[...] = acc.astype(o_ref.dtype)
        return

    @pl.when(j < n_rot_blocks)
    def _():
        cos, sa, sb = cos_ref[...], sa_ref[...], sb_ref[...]
        for c in range(acc.shape[1] // LANES):
            t = acc[:, c * LANES:(c + 1) * LANES]
            half = ROT_DIM // 2
            rot = t * cos + pltpu.roll(t, LANES - half, axis=1) * sa + pltpu.roll(t, half, axis=1) * sb
            o_ref[:, c * LANES:(c + 1) * LANES] = rot.astype(o_ref.dtype)

    @pl.when(j >= n_rot_blocks)
    def _():
        o_ref[...] = acc.astype(o_ref.dtype)


def nm_matmul(x, g, shift, scale, w, out_dtype, tm, tn, rot_tables=None, n_rot_cols=0):
    S, D = x.shape
    N = w.shape[1]
    row = lambda i, j: (i, 0)
    vec = lambda i, j: (0, 0)
    in_specs = [pl.BlockSpec((tm, D), row), pl.BlockSpec((1, D), vec), pl.BlockSpec((1, D), vec),
                pl.BlockSpec((1, D), vec), pl.BlockSpec((D, tn), lambda i, j: (0, j))]
    args = [x, g, shift, scale, w]
    if rot_tables is not None:
        in_specs += [pl.BlockSpec((tm, LANES), row)] * 3
        args += list(rot_tables)
    return pl.pallas_call(
        functools.partial(_nm_matmul_kernel, n_rot_blocks=n_rot_cols // tn),
        out_shape=jax.ShapeDtypeStruct((S, N), out_dtype),
        grid=(S // tm, N // tn),
        in_specs=in_specs,
        out_specs=pl.BlockSpec((tm, tn), lambda i, j: (i, j)),
        scratch_shapes=[pltpu.VMEM((tm, D), BF16)],
        compiler_params=_cparams(("parallel", "arbitrary")),
        name="norm_mod_matmul",
    )(*args)


def _proj_res_kernel(*refs, n_a):
    a_refs, w_refs = refs[:n_a], refs[n_a:2 * n_a]
    x_ref, gt_ref, o_ref = refs[2 * n_a:]
    acc = _dot(a_refs[0][...], w_refs[0][...])
    for a_ref, w_ref in zip(a_refs[1:], w_refs[1:]):
        acc = acc + _dot(a_ref[...], w_ref[...])
    o_ref[...] = x_ref[...] + gt_ref[...] * acc


def proj_residual(a_list, w, x, gate, tm, tn):
    S, N = x.shape
    n_a = len(a_list)
    in_specs, args = [], []
    for a in a_list:
        in_specs.append(pl.BlockSpec((tm, a.shape[1]), lambda i, j: (i, 0)))
        args.append(a)
    off = 0
    for a in a_list:
        ka = a.shape[1]
        in_specs.append(pl.BlockSpec((ka, tn), functools.partial(lambda i, j, b: (b, j), b=off // ka)))
        args.append(w)
        off += ka
    in_specs += [pl.BlockSpec((tm, tn), lambda i, j: (i, j)), pl.BlockSpec((1, tn), lambda i, j: (0, j))]
    args += [x, gate]
    return pl.pallas_call(
        functools.partial(_proj_res_kernel, n_a=n_a),
        out_shape=jax.ShapeDtypeStruct((S, N), F32),
        grid=(S // tm, N // tn),
        in_specs=in_specs,
        out_specs=pl.BlockSpec((tm, tn), lambda i, j: (i, j)),
        compiler_params=_cparams(("parallel", "parallel")),
        name="proj_residual",
    )(*args)


def _ffn_kernel(x_ref, g_ref, sh_ref, sc_ref, gt_ref, wg_ref, wu_ref, wo_ref, fg_ref, o_ref, h_ref, acc_ref,
                *, final_norm):
    k = pl.program_id(1)

    @pl.when(k == 0)
    def _():
        h_ref[...] = _norm_mod(x_ref[...], g_ref[...], sh_ref[...], sc_ref[...]).astype(BF16)
        acc_ref[...] = jnp.zeros_like(acc_ref)

    h = h_ref[...]
    gate = _dot(h, wg_ref[...])
    up = _dot(h, wu_ref[...])
    act = (gate * _sigmoid(gate) * up).astype(BF16)
    acc_ref[...] += _dot(act, wo_ref[...])

    @pl.when(k == pl.num_programs(1) - 1)
    def _():
        xn = x_ref[...] + gt_ref[...] * acc_ref[...]
        if final_norm:
            ms = jnp.mean(xn * xn, axis=-1, keepdims=True)
            xn = xn * lax.rsqrt(ms + RMS_EPS) * fg_ref[...]
        o_ref[...] = xn


def ffn(x, g, shift, scale, gate, w_in, w_out, final_g, final_norm, tm=512, th=512):
    S, D = x.shape
    H = w_out.shape[0]
    nk = H // th
    row = lambda i, k: (i, 0)
    vec = lambda i, k: (0, 0)
    return pl.pallas_call(
        functools.partial(_ffn_kernel, final_norm=final_norm),
        out_shape=jax.ShapeDtypeStruct((S, D), F32),
        grid=(S // tm, nk),
        in_specs=[pl.BlockSpec((tm, D), row), pl.BlockSpec((1, D), vec), pl.BlockSpec((1, D), vec),
                  pl.BlockSpec((1, D), vec), pl.BlockSpec((1, D), vec),
                  pl.BlockSpec((D, th), lambda i, k: (0, k)),
                  pl.BlockSpec((D, th), lambda i, k: (0, k + nk)),
                  pl.BlockSpec((th, D), lambda i, k: (k, 0)),
                  pl.BlockSpec((1, D), vec)],
        out_specs=pl.BlockSpec((tm, D), row),
        scratch_shapes=[pltpu.VMEM((tm, D), BF16), pltpu.VMEM((tm, D), F32)],
        compiler_params=_cparams(("parallel", "arbitrary")),
        name="swiglu_ffn",
    )(x, g, shift, scale, gate, w_in, w_in, w_out, final_g)


def _segsum_heads(x, ones_bd):
    cols = [_dot(x[:, c * LANES:(c + 1) * LANES], ones_bd, HI) for c in range(x.shape[1] // LANES)]
    return jnp.concatenate(cols, axis=1)


def _head_ones():
    r = lax.broadcasted_iota(jnp.int32, (LANES, LANES), 0) // HEAD_DIM
    c = lax.broadcasted_iota(jnp.int32, (LANES, LANES), 1) // HEAD_DIM
    return jnp.where(r == c, 1.0, 0.0).astype(F32)


def _rwkv_prep_kernel(*refs, has_vres, tm):
    (p_ref, pp_ref, mu_ref, w0_ref, a0_ref, kk_w_ref, ka_ref, w2_ref, a2_ref, g2_ref) = refs[:10]
    if has_vres:
        vf_ref, v0_ref, v1_ref, v2_ref = refs[10:14]
        outs = refs[14:]
    else:
        outs = refs[10:]
    r_o, k_o, v_o, kk_o, b_o, cum_o, g_o = outs
    first = pl.program_id(0) == 0
    row = lax.broadcasted_iota(jnp.int32, (tm, 1), 0)

    def mixed(c0, c1):
        p = p_ref[:, c0:c1]
        prev_last = jnp.where(first, 0.0, pp_ref[7:8, c0:c1])
        sh = jnp.where(row == 0, prev_last, pltpu.roll(p, 1, axis=0))
        return p + (sh - p) * mu_ref[:, c0:c1]

    C = RWKV_WIDTH
    r = mixed(0, C)
    k = mixed(C, 2 * C)
    v = mixed(2 * C, 3 * C)
    wa = mixed(3 * C, 3 * C + LANES)
    gl = mixed(3 * C + LANES, RWKV_IN_PAD)

    wlog = w0_ref[...] + _dot(jnp.tanh(wa), w2_ref[...], HI)
    xs = -wlog
    softplus = jnp.maximum(xs, 0.0) + jnp.log(1.0 + jnp.exp(-jnp.abs(xs)))
    w = -softplus - 0.5
    log_decay = -jnp.exp(w)
    a = _sigmoid(a0_ref[...] + _dot(wa, a2_ref[...], HI))
    g = _dot(_sigmoid(gl), g2_ref[...], HI)
    if has_vres:
        lora = _dot(_dot(v, v1_ref[...], HI), v2_ref[...], HI)
        v = v + (vf_ref[...] - v) * _sigmoid(v0_ref[...] + lora)

    kk = k * kk_w_ref[...]
    nrm = jnp.sqrt(_segsum_heads(kk * kk, _head_ones()))
    kk = kk / jnp.maximum(nrm, 1e-12)
    k = k * (1.0 + (a - 1.0) * ka_ref[...])

    ti = lax.broadcasted_iota(jnp.int32, (tm, tm), 0)
    si = lax.broadcasted_iota(jnp.int32, (tm, tm), 1)
    tri = jnp.where((si <= ti) & (si // CHUNK == ti // CHUNK), 1.0, 0.0).astype(F32)
    cum = _dot(tri, log_decay, HI)

    r_o[...] = r
    k_o[...] = k
    v_o[...] = v
    kk_o[...] = kk
    b_o[...] = kk * a
    cum_o[...] = cum
    g_o[...] = g


def rwkv_prep(p, mu, w0, a0, k_k, k_a, w2p, a2p, g2p, vres, tm=256):
    S = p.shape[0]
    C = RWKV_WIDTH
    row = lambda i: (i, 0)
    vec = lambda i: (0, 0)
    in_specs = [pl.BlockSpec((tm, RWKV_IN_PAD), row),
                pl.BlockSpec((8, RWKV_IN_PAD), lambda i: (jnp.maximum(i * (tm // 8) - 1, 0), 0)),
                pl.BlockSpec((1, RWKV_IN_PAD), vec),
                pl.BlockSpec((1, C), vec), pl.BlockSpec((1, C), vec), pl.BlockSpec((1, C), vec),
                pl.BlockSpec((1, C), vec),
                pl.BlockSpec((LANES, C), vec), pl.BlockSpec((LANES, C), vec), pl.BlockSpec((2 * LANES, C), vec)]
    args = [p, p, mu, w0, a0, k_k, k_a, w2p, a2p, g2p]
    if vres is not None:
        v_first, v0, v1p, v2p = vres
        in_specs += [pl.BlockSpec((tm, C), row), pl.BlockSpec((1, C), vec),
                     pl.BlockSpec((C, LANES), vec), pl.BlockSpec((LANES, C), vec)]
        args += [v_first, v0, v1p, v2p]
    out = jax.ShapeDtypeStruct((S, C), F32)
    return pl.pallas_call(
        functools.partial(_rwkv_prep_kernel, has_vres=vres is not None, tm=tm),
        out_shape=[out] * 7,
        grid=(S // tm,),
        in_specs=in_specs,
        out_specs=[pl.BlockSpec((tm, C), row)] * 7,
        compiler_params=_cparams(("parallel",)),
        name="rwkv_prep",
    )(*args)


def _rwkv_chunk_kernel(r_ref, k_ref, v_ref, kk_ref, b_ref, cum_ref, g_ref, rk_ref, lnw_ref, lnb_ref,
                       o_ref, ht_ref):
    L = CHUNK

    @pl.when(pl.program_id(1) == 0)
    def _():
        ht_ref[...] = jnp.zeros_like(ht_ref)

    r, k, v, kk, b, cum = r_ref[...], k_ref[...], v_ref[...], kk_ref[...], b_ref[...], cum_ref[...]
    row = lax.broadcasted_iota(jnp.int32, (L, LANES), 0)
    lane = lax.broadcasted_iota(jnp.int32, (L, LANES), 1)
    cum_excl = jnp.where(row == 0, 0.0, pltpu.roll(cum, 1, axis=0))
    e_pos = jnp.exp(cum)
    e_neg = jnp.exp(-cum)
    rt = r * e_pos
    at = -kk * jnp.exp(cum_excl)
    bt = b * e_neg
    kt = k * e_neg
    p_last = e_pos[L - 1:L, :]

    head0 = lane < HEAD_DIM

    def stack(x):
        return jnp.concatenate([jnp.where(head0, x, 0.0), jnp.where(head0, 0.0, x)], axis=0)

    s_a, s_r, s_b, s_k, s_v = stack(at), stack(rt), stack(bt), stack(kt), stack(v)
    big = _dot_nt(jnp.concatenate([s_a, s_r], axis=0), jnp.concatenate([s_b, s_k], axis=0), HI)
    ri = lax.broadcasted_iota(jnp.int32, (2 * L, 2 * L), 0)
    ci = lax.broadcasted_iota(jnp.int32, (2 * L, 2 * L), 1)
    same = (ri // L) == (ci // L)
    strict = same & ((ci % L) < (ri % L))
    incl = same & ((ci % L) <= (ri % L))
    a_ab = jnp.where(strict, big[:2 * L, :2 * L], 0.0)
    a_ak = jnp.where(strict, big[:2 * L, 2 * L:], 0.0)
    r_b = jnp.where(incl, big[2 * L:, :2 * L], 0.0)
    r_k = jnp.where(incl, big[2 * L:, 2 * L:], 0.0)

    eye = jnp.where(ri == ci, 1.0, 0.0).astype(F32)
    t_inv = eye + a_ab
    pw = a_ab
    for _ in range(int(math.log2(L)) - 1):
        pw = _dot(pw, pw, HI)
        t_inv = t_inv + _dot(t_inv, pw, HI)

    w_s = _dot(t_inv, s_a, HI)
    u_s0 = _dot(t_inv, _dot(a_ak, s_v, HI), HI)
    ht = ht_ref[...]
    u_s = _dot_nt(w_s, ht, HI) + u_s0
    y_s = _dot_nt(s_r, ht, HI) + _dot(r_b, u_s, HI) + _dot(r_k, s_v, HI)
    upd = _dot_tn(jnp.concatenate([u_s, s_v], axis=0), jnp.concatenate([s_b, s_k], axis=0), HI)
    ht_ref[...] = (ht + upd) * p_last
    y = y_s[:L] + y_s[L:]

    ones_bd = _head_ones()
    inv_n = 1.0 / HEAD_DIM
    mean = _dot(y, ones_bd, HI) * inv_n
    yc = y - mean
    var = _dot(yc * yc, ones_bd, HI) * inv_n
    yn = yc * lax.rsqrt(var + GN_EPS) * lnw_ref[...] + lnb_ref[...]
    bonus = _dot(r * k * rk_ref[...], ones_bd, HI)
    o_ref[...] = ((yn + bonus * v) * g_ref[...]).astype(o_ref.dtype)


def rwkv_chunk(r, k, v, kk, b, cum, g, r_k, ln_w, ln_b):
    S, C = r.shape
    blk = pl.BlockSpec((CHUNK, LANES), lambda h, c: (c, h))
    vec = pl.BlockSpec((1, LANES), lambda h, c: (0, h))
    return pl.pallas_call(
        _rwkv_chunk_kernel,
        out_shape=jax.ShapeDtypeStruct((S, C), BF16),
        grid=(C // LANES, S // CHUNK),
        in_specs=[blk] * 7 + [vec] * 3,
        out_specs=blk,
        scratch_shapes=[pltpu.VMEM((LANES, LANES), F32)],
        compiler_params=_cparams(("parallel", "arbitrary")),
        name="rwkv_chunk",
    )(r, k, v, kk, b, cum, g, r_k, ln_w, ln_b)


def _gelu_tanh(y):
    return 0.5 * y * (1.0 + jnp.tanh(math.sqrt(2.0 / math.pi) * (y + 0.044715 * (y * y * y))))


def _s5_kernel(u_ref, bm_ref, cm_ref, pw_ref, ct_ref, d_ref, o_ref, carry_ref, *, tm):
    half = 8 * S5_STATE

    @pl.when(pl.program_id(1) == 0)
    def _():
        carry_ref[...] = jnp.zeros_like(carry_ref)

    u = u_ref[...]
    bu = _dot(u, bm_ref[0], HI)
    xr, xi = bu[:, :half], bu[:, half:]
    row = lax.broadcasted_iota(jnp.int32, (tm, 1), 0)
    for lvl in range(int(math.log2(tm))):
        off = 1 << lvl
        ar, ai = pw_ref[0, lvl:lvl + 1, :half], pw_ref[0, lvl:lvl + 1, half:]
        keep = row >= off
        sr = jnp.where(keep, pltpu.roll(xr, off, axis=0), 0.0)
        si = jnp.where(keep, pltpu.roll(xi, off, axis=0), 0.0)
        xr, xi = xr + ar * sr - ai * si, xi + ar * si + ai * sr
    cr, ci = carry_ref[:, :half], carry_ref[:, half:]
    tr, ti = ct_ref[0, :, :half], ct_ref[0, :, half:]
    xr, xi = xr + tr * cr - ti * ci, xi + tr * ci + ti * cr
    carry_ref[:, :half] = xr[tm - 1:tm, :]
    carry_ref[:, half:] = xi[tm - 1:tm, :]
    y = _dot(jnp.concatenate([xr, xi], axis=1), cm_ref[0], HI) + d_ref[...] * u
    o_ref[...] = _gelu_tanh(y)


def s5_scan(p, col0, bmat, cmat, pows, ctab, d, tm=128):
    S = p.shape[0]
    nj = S5_WIDTH // LANES
    nst = 2 * 8 * S5_STATE
    nlev = pows.shape[1]
    return pl.pallas_call(
        functools.partial(_s5_kernel, tm=tm),
        out_shape=jax.ShapeDtypeStruct((S, S5_WIDTH), F32),
        grid=(nj, S // tm),
        in_specs=[pl.BlockSpec((tm, LANES), lambda j, i: (i, col0 // LANES + j)),
                  pl.BlockSpec((1, LANES, nst), lambda j, i: (j, 0, 0)),
                  pl.BlockSpec((1, nst, LANES), lambda j, i: (j, 0, 0)),
                  pl.BlockSpec((1, nlev, nst), lambda j, i: (j, 0, 0)),
                  pl.BlockSpec((1, tm, nst), lambda j, i: (j, 0, 0)),
                  pl.BlockSpec((1, LANES), lambda j, i: (0, j))],
        out_specs=pl.BlockSpec((tm, LANES), lambda j, i: (i, j)),
        scratch_shapes=[pltpu.VMEM((1, nst), F32)],
        compiler_params=_cparams(("parallel", "arbitrary")),
        name="s5_scan",
    )(p, bmat, cmat, pows, ctab, d)


def _s5_tables(lam_re, lam_im, log_step, b_re, b_im, c_re, c_im, tm):
    G, P, GS = S5_WIDTH // S5_GROUP, S5_STATE, S5_GROUP
    nj = G // 8
    lam = lax.complex(jnp.minimum(lam_re, -1e-4), lam_im)
    step = jnp.exp(log_step)[:, None]
    lam_dt = lam * step
    lam_bar = jnp.exp(lam_dt)
    b_bar = ((lam_bar - 1.0) / lam)[..., None] * lax.complex(b_re, b_im)

    def pack(z):
        z = jnp.moveaxis(z, -2, 0).reshape((nj, 8) + z.shape[:-2] + (P,))
        z = jnp.moveaxis(z, 1, -2)
        z = z.reshape(z.shape[:-2] + (8 * P,))
        return jnp.concatenate([jnp.real(z), jnp.imag(z)], axis=-1).astype(F32)

    nlev = int(math.log2(tm))
    offs = (2.0 ** jnp.arange(nlev, dtype=F32))[:, None, None]
    pows = pack(jnp.exp(lam_dt[None] * offs))
    steps = jnp.arange(1, tm + 1, dtype=F32)[:, None, None]
    ctab = pack(jnp.exp(lam_dt[None] * steps))

    eye8 = jnp.eye(8, dtype=F32)
    bb = b_bar.reshape(nj, 8, P, GS)
    def bd_in(x):
        return jnp.einsum('jgph,gk->jghkp', x, eye8).reshape(nj, 8 * GS, 8 * P)
    bmat = jnp.concatenate([bd_in(jnp.real(bb)), bd_in(jnp.imag(bb))], axis=-1).astype(F32)
    def bd_out(x):
        return jnp.einsum('jghp,gk->jkpgh', x, eye8).reshape(nj, 8 * P, 8 * GS)
    cmat = jnp.concatenate([bd_out(c_re.reshape(nj, 8, GS, P)), -bd_out(c_im.reshape(nj, 8, GS, P))],
                           axis=1).astype(F32)
    return bmat, cmat, pows, ctab


def _glu_kernel(y_ref, yc_ref, w_ref, b_ref, o_ref, a_ref):
    @pl.when(pl.program_id(1) == 0)
    def _():
        a_ref[...] = y_ref[...].astype(BF16)

    z = _dot(a_ref[...], w_ref[...]) + b_ref[...]
    o_ref[...] = (yc_ref[...] * _sigmoid(z)).astype(o_ref.dtype)


def glu(y, w, b, tm=512, tn=512):
    S, C = y.shape
    return pl.pallas_call(
        _glu_kernel,
        out_shape=jax.ShapeDtypeStruct((S, C), BF16),
        grid=(S // tm, C // tn),
        in_specs=[pl.BlockSpec((tm, C), lambda i, j: (i, 0)), pl.BlockSpec((tm, tn), lambda i, j: (i, j)),
                  pl.BlockSpec((C, tn), lambda i, j: (0, j)), pl.BlockSpec((1, tn), lambda i, j: (0, j))],
        out_specs=pl.BlockSpec((tm, tn), lambda i, j: (i, j)),
        scratch_shapes=[pltpu.VMEM((tm, C), BF16)],
        compiler_params=_cparams(("parallel", "arbitrary")),
        name="s5_glu",
    )(y, y, w, b)


def _attn_kernel(q_ref, kc_ref, kp_ref, vc_ref, vp_ref, o_ref, lse_ref, *, tq):
    B = ATTN_BLK
    first_tile = pl.program_id(1) == 0
    q = q_ref[...] * (HEAD_DIM ** -0.5)
    kf = jnp.concatenate([kp_ref[...], kc_ref[...]], axis=0)
    vf = jnp.concatenate([vp_ref[...], vc_ref[...]], axis=0)
    lane = lax.broadcasted_iota(jnp.int32, (B, LANES), 1)
    head0 = lane < HEAD_DIM
    qi = lax.broadcasted_iota(jnp.int32, (B, 2 * B), 0)
    kj = lax.broadcasted_iota(jnp.int32, (B, 2 * B), 1)
    band = (kj >= qi) & (kj <= qi + B)
    zero = jnp.zeros((), q.dtype)
    for qb in range(tq // B):
        qs = q[qb * B:(qb + 1) * B]
        ks = kf[qb * B:(qb + 2) * B]
        vs = vf[qb * B:(qb + 2) * B]
        mask = band
        if qb == 0:
            mask = band & (kj >= jnp.where(first_tile, B, 0))
        outs, lses = [], []
        for hmask in (head0, jnp.logical_not(head0)):
            s = _dot_nt(jnp.where(hmask, qs, zero), ks)
            s = jnp.where(mask, s, -jnp.inf)
            m = jnp.max(s, axis=-1, keepdims=True)
            pexp = jnp.exp(s - m)
            den = jnp.sum(pexp, axis=-1, keepdims=True)
            outs.append(_dot(pexp.astype(vs.dtype), vs) / den)
            lses.append(m + jnp.log(den))
        o_ref[qb * B:(qb + 1) * B, :] = jnp.where(head0, outs[0], outs[1])
        lse_ref[qb * B:(qb + 1) * B, :] = jnp.where(head0, lses[0], lses[1])


def attn_branch(qkv, dilation, tq):
    S = qkv.shape[0]
    d = dilation
    W3 = 3 * D_MODEL
    Sd = S // d
    tq = min(tq, Sd)
    view = qkv.reshape(Sd, d * W3)
    nb = W3 // LANES
    nh = D_MODEL // LANES
    rpb = tq // ATTN_BLK

    def cur(part):
        return pl.BlockSpec((tq, LANES), lambda c, i: (i, (c // nh) * nb + part * nh + c % nh))

    def prev(part):
        return pl.BlockSpec((ATTN_BLK, LANES),
                            lambda c, i: (jnp.maximum(i * rpb - 1, 0), (c // nh) * nb + part * nh + c % nh))

    out = jax.ShapeDtypeStruct((Sd, d * D_MODEL), F32)
    o, lse = pl.pallas_call(
        functools.partial(_attn_kernel, tq=tq),
        out_shape=[out, out],
        grid=(d * nh, Sd // tq),
        in_specs=[cur(0), cur(1), prev(1), cur(2), prev(2)],
        out_specs=[pl.BlockSpec((tq, LANES), lambda c, i: (i, c))] * 2,
        compiler_params=_cparams(("parallel", "arbitrary")),
        name="dilated_attn_d%d" % d,
    )(view, view, view, view, view)
    return o.reshape(S, D_MODEL), lse.reshape(S, D_MODEL)


def _merge_kernel(o1, o2, o3, l1, l2, l3, out_ref):
    a, b, c = l1[...], l2[...], l3[...]
    m = jnp.maximum(jnp.maximum(a, b), c)
    ea, eb, ec = jnp.exp(a - m), jnp.exp(b - m), jnp.exp(c - m)
    tot = ea + eb + ec
    out_ref[...] = ((ea * o1[...] + eb * o2[...] + ec * o3[...]) / tot).astype(out_ref.dtype)


def merge_branches(outs, lses, tm=512, tn=1024):
    S, D = outs[0].shape
    blk = pl.BlockSpec((tm, tn), lambda i, j: (i, j))
    return pl.pallas_call(
        _merge_kernel,
        out_shape=jax.ShapeDtypeStruct((S, D), BF16),
        grid=(S // tm, D // tn),
        in_specs=[blk] * 6,
        out_specs=blk,
        compiler_params=_cparams(("parallel", "parallel")),
        name="attn_merge",
    )(*outs, *lses)


def _rotary_tables(S):
    half = ROT_DIM // 2
    inv = ROPE_THETA ** (-jnp.arange(half, dtype=F32) * 2.0 / ROT_DIM)
    ang = jnp.arange(S, dtype=F32)[:, None] * inv[None, :]
    cos, sin = jnp.cos(ang), jnp.sin(ang)
    rest = HEAD_DIM - ROT_DIM
    ones = jnp.ones((S, rest), F32)
    zeros = jnp.zeros((S, rest), F32)
    zh = jnp.zeros((S, half), F32)
    cos_t = jnp.concatenate([cos, cos, ones], axis=1)
    sin_a = jnp.concatenate([-sin, zh, zeros], axis=1)
    sin_b = jnp.concatenate([zh, sin, zeros], axis=1)
    rep = LANES // HEAD_DIM
    return tuple(jnp.tile(t, (1, rep)) for t in (cos_t, sin_a, sin_b))


def _pad_rows(w, n):
    return jnp.pad(w, ((0, n - w.shape[0]), (0, 0)))


def kernel(x, c, ada_w, ada_b, norm_mix_g, norm_ffn_g, hyb_w_in, hyb_w_out, rwkv_mu, rwkv_w0, rwkv_w2, rwkv_a0, rwkv_a2, rwkv_g2, rwkv_k_k, rwkv_k_a, rwkv_r_k, rwkv_ln_w, rwkv_ln_b, rwkv_v0, rwkv_v1, rwkv_v2, s5_lam_re, s5_lam_im, s5_log_step, s5_b_re, s5_b_im, s5_c_re, s5_c_im, s5_d, s5_glu_w, s5_glu_b, attn_w_qkv, attn_w_o, ffn_w_in, ffn_w_out, final_norm_g):
    B, S, D = x.shape
    assert B == 1 and D == D_MODEL and S % 2048 == 0
    C = RWKV_WIDTH
    xs = x.reshape(S, D)
    mod = ada_modulation(c, ada_w, ada_b)
    rot_tables = _rotary_tables(S)
    s5_tm = 128
    v_first = None
    row = lambda t: t.reshape(1, -1)

    for i in range(DEPTH):
        m = mod[i]
        sh_mix, sc_mix, gt_mix, sh_ffn, sc_ffn, gt_ffn = [m[:, q * D:(q + 1) * D] for q in range(6)]
        if i % 2 == 0:
            j = i // 2
            w_in = hyb_w_in[j]
            gpad = RWKV_IN_PAD - RWKV_IN
            w_in = jnp.concatenate([w_in[:, :RWKV_IN], jnp.zeros((D, gpad), F32), w_in[:, RWKV_IN:]],
                                   axis=1).astype(BF16)
            mu = jnp.concatenate([rwkv_mu[j], jnp.zeros((gpad,), F32)]).reshape(1, RWKV_IN_PAD)
            p = nm_matmul(xs, row(norm_mix_g[i]), sh_mix, sc_mix, w_in, F32, tm=512, tn=640)

            w2p = jnp.concatenate([rwkv_w2[j], jnp.zeros((LORA_A, C), F32)], axis=0)
            a2p = jnp.concatenate([jnp.zeros((LORA_W, C), F32), rwkv_a2[j]], axis=0)
            g2p = _pad_rows(rwkv_g2[j], 2 * LANES)
            vres = None
            if j > 0:
                v1p = jnp.pad(rwkv_v1[j - 1], ((0, 0), (0, LANES - LORA_V)))
                v2p = _pad_rows(rwkv_v2[j - 1], LANES)
                vres = (v_first, row(rwkv_v0[j - 1]), v1p, v2p)
            r, k, v, kk, b, cum, g = rwkv_prep(p, mu, row(rwkv_w0[j]), row(rwkv_a0[j]), row(rwkv_k_k[j]),
                                               row(rwkv_k_a[j]), w2p, a2p, g2p, vres)
            if j == 0:
                v_first = v
            y_rwkv = rwkv_chunk(r, k, v, kk, b, cum, g, row(rwkv_r_k[j]), row(rwkv_ln_w[j]),
                                row(rwkv_ln_b[j]))

            bmat, cmat, pows, ctab = _s5_tables(s5_lam_re[j], s5_lam_im[j], s5_log_step[j], s5_b_re[j],
                                                s5_b_im[j], s5_c_re[j], s5_c_im[j], s5_tm)
            y_s5 = s5_scan(p, RWKV_IN_PAD, bmat, cmat, pows, ctab, row(s5_d[j]), tm=s5_tm)
            y_s5 = glu(y_s5, s5_glu_w[j].astype(BF16), row(s5_glu_b[j]))
            xs = proj_residual([y_rwkv, y_s5], hyb_w_out[j].astype(BF16), xs, gt_mix, tm=512, tn=512)
        else:
            j = i // 2
            qkv = nm_matmul(xs, row(norm_mix_g[i]), sh_mix, sc_mix, attn_w_qkv[j].astype(BF16), BF16,
                            tm=512, tn=512, rot_tables=rot_tables, n_rot_cols=2 * D)
            outs, lses = zip(*[attn_branch(qkv, d, tq=512) for (_, d) in DILATED_BRANCHES])
            o = merge_branches(outs, lses)
            xs = proj_residual([o], attn_w_o[j].astype(BF16), xs, gt_mix, tm=512, tn=512)
        xs = ffn(xs, row(norm_ffn_g[i]), sh_ffn, sc_ffn, gt_ffn, ffn_w_in[i].astype(BF16),
                 ffn_w_out[i].astype(BF16), row(final_norm_g), final_norm=(i == DEPTH - 1))
    return xs.reshape(B, S, D)
```

```python
import functools
import math

import jax
import jax.numpy as jnp
from jax import lax
from jax.experimental import pallas as pl
from jax.experimental.pallas import tpu as pltpu

F32 = jnp.float32
BF16 = jnp.bfloat16
HI = lax.Precision.HIGHEST

D_MODEL = 2048
DEPTH = 4
HEAD_DIM = 64
RWKV_WIDTH = 1024
LORA_W = 64
LORA_A = 64
LORA_V = 32
LORA_G = 160
RWKV_IN = 3 * RWKV_WIDTH + LORA_W + LORA_A + LORA_G
RWKV_IN_PAD = 3456
S5_WIDTH = 1024
S5_GROUP = 16
S5_STATE = 64
S5_LAG = 16
EVEN_IN_PAD = RWKV_IN_PAD + S5_WIDTH
ATTN_HEADS = 32
ROT_DIM = 16
ROPE_THETA = 500000.0
DILATED_BRANCHES = ((128, 1), (512, 4), (2048, 16))
ATTN_BLK = 128
FFN_HIDDEN = 5632
RMS_EPS = 1e-6
GN_EPS = 64e-5

LANES = 128
CHUNK = 64
VMEM_LIMIT = 48 * 1024 * 1024


def _cparams(sem):
    return pltpu.CompilerParams(dimension_semantics=sem, vmem_limit_bytes=VMEM_LIMIT)


def _sigmoid(x):
    return 1.0 / (1.0 + jnp.exp(-x))


def _dot(a, b, prec=None):
    return jnp.dot(a, b, preferred_element_type=F32, precision=prec)


def _dot_nt(a, b, prec=None):
    return lax.dot_general(a, b, (((1,), (1,)), ((), ())), preferred_element_type=F32, precision=prec)


def _dot_tn(a, b, prec=None):
    return lax.dot_general(a, b, (((0,), (0,)), ((), ())), preferred_element_type=F32, precision=prec)


def _norm_mod(x, g, shift, scale):
    ms = jnp.mean(x * x, axis=-1, keepdims=True)
    return (x * lax.rsqrt(ms + RMS_EPS) * g) * (1.0 + scale) + shift


def _ada_kernel(c_ref, w_ref, b_ref, o_ref):
    c = c_ref[...]
    s = c * _sigmoid(c)
    o_ref[0] = jnp.sum(s * w_ref[0], axis=0, keepdims=True) + b_ref[0]


def ada_modulation(c, ada_w, ada_b, tn=1024):
    L, D, N = ada_w.shape
    return pl.pallas_call(
        _ada_kernel,
        out_shape=jax.ShapeDtypeStruct((L, 1, N), F32),
        grid=(L, N // tn),
        in_specs=[pl.BlockSpec((D, 1), lambda l, j: (0, 0)),
                  pl.BlockSpec((1, D, tn), lambda l, j: (l, 0, j)),
                  pl.BlockSpec((1, 1, tn), lambda l, j: (l, 0, j))],
        out_specs=pl.BlockSpec((1, 1, tn), lambda l, j: (l, 0, j)),
        compiler_params=_cparams(("parallel", "parallel")),
        name="ada_modulation",
    )(c.reshape(D, 1), ada_w, ada_b.reshape(L, 1, N))


def _nm_matmul_kernel(x_ref, g_ref, sh_ref, sc_ref, w_ref, o_ref, h_ref):
    @pl.when(pl.program_id(1) == 0)
    def _():
        h_ref[...] = _norm_mod(x_ref[...], g_ref[...], sh_ref[...], sc_ref[...]).astype(BF16)

    o_ref[...] = _dot(h_ref[...], w_ref[...]).astype(o_ref.dtype)


def _nm_specs(x, w, tm, tn):
    D = x.shape[1]
    row = lambda i, j: (i, 0)
    vec = lambda i, j: (0, 0)
    return [pl.BlockSpec((tm, D), row), pl.BlockSpec((1, D), vec), pl.BlockSpec((1, D), vec),
            pl.BlockSpec((1, D), vec), pl.BlockSpec((D, tn), lambda i, j: (0, j))]


def nm_matmul(x, g, shift, scale, w, out_dtype, tm, tn):
    S, D = x.shape
    N = w.shape[1]
    return pl.pallas_call(
        _nm_matmul_kernel,
        out_shape=jax.ShapeDtypeStruct((S, N), out_dtype),
        grid=(S // tm, N // tn),
        in_specs=_nm_specs(x, w, tm, tn),
        out_specs=pl.BlockSpec((tm, tn), lambda i, j: (i, j)),
        scratch_shapes=[pltpu.VMEM((tm, D), BF16)],
        compiler_params=_cparams(("parallel", "arbitrary")),
        name="norm_mod_matmul",
    )(x, g, shift, scale, w)


def _qkv_kernel(x_ref, g_ref, sh_ref, sc_ref, w_ref, cos_ref, sa_ref, sb_ref, *rest, n_rot_blocks, dilations):
    o_refs, (h_ref, t_ref) = rest[:len(dilations)], rest[len(dilations):]
    j = pl.program_id(1)
    n_lane_blocks, tm, _ = t_ref.shape

    @pl.when(j == 0)
    def _():
        h_ref[...] = _norm_mod(x_ref[...], g_ref[...], sh_ref[...], sc_ref[...]).astype(BF16)

    acc = _dot(h_ref[...], w_ref[...])

    @pl.when(j < n_rot_blocks)
    def _():
        cos, sa, sb = cos_ref[...], sa_ref[...], sb_ref[...]
        half = ROT_DIM // 2
        for c in range(n_lane_blocks):
            t = acc[:, c * LANES:(c + 1) * LANES]
            t_ref[c] = t * cos + pltpu.roll(t, LANES - half, axis=1) * sa + pltpu.roll(t, half, axis=1) * sb

    @pl.when(j >= n_rot_blocks)
    def _():
        for c in range(n_lane_blocks):
            t_ref[c] = acc[:, c * LANES:(c + 1) * LANES]

    for d, o_ref in zip(dilations, o_refs):
        for c in range(n_lane_blocks):
            cs = slice(c * LANES, (c + 1) * LANES)
            if d == 1:
                o_ref[0, :, cs] = t_ref[c].astype(o_ref.dtype)
                continue
            for r in range(d):
                o_ref[r, :, cs] = t_ref[c, pl.ds(r, tm // d, stride=d), :].astype(o_ref.dtype)


def qkv_proj(x, g, shift, scale, w, rot_tables, n_rot_cols, dilations, tm, tn):
    S, D = x.shape
    N = w.shape[1]
    in_specs = _nm_specs(x, w, tm, tn) + [pl.BlockSpec((tm, LANES), lambda i, j: (i, 0))] * 3
    return pl.pallas_call(
        functools.partial(_qkv_kernel, n_rot_blocks=n_rot_cols // tn, dilations=dilations),
        out_shape=[jax.ShapeDtypeStruct((d, S // d, N), BF16) for d in dilations],
        grid=(S // tm, N // tn),
        in_specs=in_specs,
        out_specs=[pl.BlockSpec((d, tm // d, tn), lambda i, j: (0, i, j)) for d in dilations],
        scratch_shapes=[pltpu.VMEM((tm, D), BF16), pltpu.VMEM((tn // LANES, tm, LANES), F32)],
        compiler_params=_cparams(("parallel", "arbitrary")),
        name="qkv_proj",
    )(x, g, shift, scale, w, *rot_tables)


def _proj_res_kernel(*refs, n_a):
    a_refs, w_refs = refs[:n_a], refs[n_a:2 * n_a]
    x_ref, gt_ref, o_ref = refs[2 * n_a:]
    acc = _dot(a_refs[0][...], w_refs[0][...])
    for a_ref, w_ref in zip(a_refs[1:], w_refs[1:]):
        acc = acc + _dot(a_ref[...], w_ref[...])
    o_ref[...] = x_ref[...] + gt_ref[...] * acc


def proj_residual(a_list, w, x, gate, tm, tn):
    S, N = x.shape
    n_a = len(a_list)
    in_specs, args = [], []
    for a in a_list:
        in_specs.append(pl.BlockSpec((tm, a.shape[1]), lambda i, j: (i, 0)))
        args.append(a)
    off = 0
    for a in a_list:
        ka = a.shape[1]
        in_specs.append(pl.BlockSpec((ka, tn), functools.partial(lambda i, j, b: (b, j), b=off // ka)))
        args.append(w)
        off += ka
    in_specs += [pl.BlockSpec((tm, tn), lambda i, j: (i, j)), pl.BlockSpec((1, tn), lambda i, j: (0, j))]
    args += [x, gate]
    return pl.pallas_call(
        functools.partial(_proj_res_kernel, n_a=n_a),
        out_shape=jax.ShapeDtypeStruct((S, N), F32),
        grid=(S // tm, N // tn),
        in_specs=in_specs,
        out_specs=pl.BlockSpec((tm, tn), lambda i, j: (i, j)),
        compiler_params=_cparams(("parallel", "parallel")),
        name="proj_residual",
    )(*args)


def _ffn_kernel(x_ref, g_ref, sh_ref, sc_ref, gt_ref, wg_ref, wu_ref, wo_ref, fg_ref, o_ref, h_ref, acc_ref,
                *, final_norm):
    k = pl.program_id(1)

    @pl.when(k == 0)
    def _():
        h_ref[...] = _norm_mod(x_ref[...], g_ref[...], sh_ref[...], sc_ref[...]).astype(BF16)
        acc_ref[...] = jnp.zeros_like(acc_ref)

    h = h_ref[...]
    gate = _dot(h, wg_ref[...])
    up = _dot(h, wu_ref[...])
    act = (gate * _sigmoid(gate) * up).astype(BF16)
    acc_ref[...] += _dot(act, wo_ref[...])

    @pl.when(k == pl.num_programs(1) - 1)
    def _():
        xn = x_ref[...] + gt_ref[...] * acc_ref[...]
        if final_norm:
            ms = jnp.mean(xn * xn, axis=-1, keepdims=True)
            xn = xn * lax.rsqrt(ms + RMS_EPS) * fg_ref[...]
        o_ref[...] = xn


def ffn(x, g, shift, scale, gate, w_in, w_out, final_g, final_norm, tm=512, th=512):
    S, D = x.shape
    H = w_out.shape[0]
    nk = H // th
    row = lambda i, k: (i, 0)
    vec = lambda i, k: (0, 0)
    return pl.pallas_call(
        functools.partial(_ffn_kernel, final_norm=final_norm),
        out_shape=jax.ShapeDtypeStruct((S, D), F32),
        grid=(S // tm, nk),
        in_specs=[pl.BlockSpec((tm, D), row), pl.BlockSpec((1, D), vec), pl.BlockSpec((1, D), vec),
                  pl.BlockSpec((1, D), vec), pl.BlockSpec((1, D), vec),
                  pl.BlockSpec((D, th), lambda i, k: (0, k)),
                  pl.BlockSpec((D, th), lambda i, k: (0, k + nk)),
                  pl.BlockSpec((th, D), lambda i, k: (k, 0)),
                  pl.BlockSpec((1, D), vec)],
        out_specs=pl.BlockSpec((tm, D), row),
        scratch_shapes=[pltpu.VMEM((tm, D), BF16), pltpu.VMEM((tm, D), F32)],
        compiler_params=_cparams(("parallel", "arbitrary")),
        name="swiglu_ffn",
    )(x, g, shift, scale, gate, w_in, w_in, w_out, final_g)


def _segsum_heads(x, ones_bd):
    cols = [_dot(x[:, c * LANES:(c + 1) * LANES], ones_bd, HI) for c in range(x.shape[1] // LANES)]
    return jnp.concatenate(cols, axis=1)


def _head_ones():
    r = lax.broadcasted_iota(jnp.int32, (LANES, LANES), 0) // HEAD_DIM
    c = lax.broadcasted_iota(jnp.int32, (LANES, LANES), 1) // HEAD_DIM
    return jnp.where(r == c, 1.0, 0.0).astype(F32)


def _rwkv_prep_kernel(*refs, has_vres, tm):
    (p_ref, pp_ref, mu_ref, w0_ref, a0_ref, kk_w_ref, ka_ref, w2_ref, a2_ref, g2_ref) = refs[:10]
    if has_vres:
        vf_ref, v0_ref, v1_ref, v2_ref = refs[10:14]
        outs = refs[14:]
    else:
        outs = refs[10:]
    r_o, k_o, v_o, kk_o, b_o, cum_o, g_o = outs
    first = pl.program_id(0) == 0
    row = lax.broadcasted_iota(jnp.int32, (tm, 1), 0)

    def mixed(c0, c1):
        p = p_ref[:, c0:c1]
        prev_last = jnp.where(first, 0.0, pp_ref[7:8, c0:c1])
        sh = jnp.where(row == 0, prev_last, pltpu.roll(p, 1, axis=0))
        return p + (sh - p) * mu_ref[:, c0:c1]

    C = RWKV_WIDTH
    r = mixed(0, C)
    k = mixed(C, 2 * C)
    v = mixed(2 * C, 3 * C)
    wa = mixed(3 * C, 3 * C + LANES)
    gl = mixed(3 * C + LANES, RWKV_IN_PAD)

    wlog = w0_ref[...] + _dot(jnp.tanh(wa), w2_ref[...], HI)
    xs = -wlog
    softplus = jnp.maximum(xs, 0.0) + jnp.log(1.0 + jnp.exp(-jnp.abs(xs)))
    w = -softplus - 0.5
    log_decay = -jnp.exp(w)
    a = _sigmoid(a0_ref[...] + _dot(wa, a2_ref[...], HI))
    g = _dot(_sigmoid(gl), g2_ref[...], HI)
    if has_vres:
        lora = _dot(_dot(v, v1_ref[...], HI), v2_ref[...], HI)
        v = v + (vf_ref[...] - v) * _sigmoid(v0_ref[...] + lora)

    kk = k * kk_w_ref[...]
    nrm = jnp.sqrt(_segsum_heads(kk * kk, _head_ones()))
    kk = kk / jnp.maximum(nrm, 1e-12)
    k = k * (1.0 + (a - 1.0) * ka_ref[...])

    ti = lax.broadcasted_iota(jnp.int32, (tm, tm), 0)
    si = lax.broadcasted_iota(jnp.int32, (tm, tm), 1)
    tri = jnp.where((si <= ti) & (si // CHUNK == ti // CHUNK), 1.0, 0.0).astype(F32)
    cum = _dot(tri, log_decay, HI)

    r_o[...] = r
    k_o[...] = k
    v_o[...] = v
    kk_o[...] = kk
    b_o[...] = kk * a
    cum_o[...] = cum
    g_o[...] = g


def rwkv_prep(p, mu, w0, a0, k_k, k_a, w2p, a2p, g2p, vres, tm=256):
    S = p.shape[0]
    C = RWKV_WIDTH
    row = lambda i: (i, 0)
    vec = lambda i: (0, 0)
    in_specs = [pl.BlockSpec((tm, RWKV_IN_PAD), row),
                pl.BlockSpec((8, RWKV_IN_PAD), lambda i: (jnp.maximum(i * (tm // 8) - 1, 0), 0)),
                pl.BlockSpec((1, RWKV_IN_PAD), vec),
                pl.BlockSpec((1, C), vec), pl.BlockSpec((1, C), vec), pl.BlockSpec((1, C), vec),
                pl.BlockSpec((1, C), vec),
                pl.BlockSpec((LANES, C), vec), pl.BlockSpec((LANES, C), vec), pl.BlockSpec((2 * LANES, C), vec)]
    args = [p, p, mu, w0, a0, k_k, k_a, w2p, a2p, g2p]
    if vres is not None:
        v_first, v0, v1p, v2p = vres
        in_specs += [pl.BlockSpec((tm, C), row), pl.BlockSpec((1, C), vec),
                     pl.BlockSpec((C, LANES), vec), pl.BlockSpec((LANES, C), vec)]
        args += [v_first, v0, v1p, v2p]
    out = jax.ShapeDtypeStruct((S, C), F32)
    return pl.pallas_call(
        functools.partial(_rwkv_prep_kernel, has_vres=vres is not None, tm=tm),
        out_shape=[out] * 7,
        grid=(S // tm,),
        in_specs=in_specs,
        out_specs=[pl.BlockSpec((tm, C), row)] * 7,
        compiler_params=_cparams(("parallel",)),
        name="rwkv_prep",
    )(*args)


_NN = (((1,), (0,)), ((), ()))
_NT = (((1,), (1,)), ((), ()))
_TN = (((0,), (0,)), ((), ()))


def _split(x):
    hi = x.astype(BF16)
    return hi, (x - hi.astype(F32)).astype(BF16)


def _mm3(a, b, dims=_NN):
    f = lambda x, y: lax.dot_general(x, y, dims, preferred_element_type=F32)
    return f(a[0], b[0]) + (f(a[0], b[1]) + f(a[1], b[0]))


def _dot_ones(x, ones_bf16):
    hi = x.astype(BF16)
    r1 = x - hi.astype(F32)
    mid = r1.astype(BF16)
    lo = (r1 - mid.astype(F32)).astype(BF16)
    f = lambda t: jnp.dot(t, ones_bf16, preferred_element_type=F32)
    return f(hi) + (f(mid) + f(lo))


def _rwkv_chunk_kernel(r_ref, k_ref, v_ref, kk_ref, b_ref, cum_ref, g_ref, rk_ref, lnw_ref, lnb_ref,
                       o_ref, ht_ref, y_ref, *, n_chunks, n_pairs):
    L = CHUNK

    @pl.when(pl.program_id(1) == 0)
    def _():
        ht_ref[...] = jnp.zeros_like(ht_ref)

    row = lax.broadcasted_iota(jnp.int32, (L, LANES), 0)
    lane = lax.broadcasted_iota(jnp.int32, (L, LANES), 1)
    head0 = lane < HEAD_DIM
    ri = lax.broadcasted_iota(jnp.int32, (2 * L, 2 * L), 0)
    ci = lax.broadcasted_iota(jnp.int32, (2 * L, 2 * L), 1)
    same = (ri // L) == (ci // L)
    strict = same & ((ci % L) < (ri % L))
    incl = same & ((ci % L) <= (ri % L))
    eye = jnp.where(ri == ci, 1.0, 0.0).astype(F32)

    def stack(x):
        return jnp.concatenate([jnp.where(head0, x, 0.0), jnp.where(head0, 0.0, x)], axis=0)

    items = [(c, p) for c in range(n_chunks) for p in range(n_pairs)]
    pre = [dict() for _ in items]
    for q, (c, p) in zip(pre, items):
        sl = (slice(c * L, (c + 1) * L), slice(p * LANES, (p + 1) * LANES))
        r, k, v, kk, b, cum = r_ref[sl], k_ref[sl], v_ref[sl], kk_ref[sl], b_ref[sl], cum_ref[sl]
        cum_excl = jnp.where(row == 0, 0.0, pltpu.roll(cum, 1, axis=0))
        e_pos = jnp.exp(cum)
        e_neg = jnp.exp(-cum)
        q["s_a"] = stack(-kk * jnp.exp(cum_excl))
        q["v"] = stack(v)
        sp_ar = _split(jnp.concatenate([q["s_a"], stack(r * e_pos)], axis=0))
        q["bk"] = _split(jnp.concatenate([stack(b * e_neg), stack(k * e_neg)], axis=0))
        q["r"] = (sp_ar[0][2 * L:], sp_ar[1][2 * L:])
        q["p_last"] = e_pos[L - 1:L, :]
        q["big"] = _mm3(sp_ar, q["bk"], _NT)
    for q in pre:
        big = q.pop("big")
        q["pw"] = jnp.where(strict, big[:2 * L, :2 * L], 0.0)
        q["a_ak"] = jnp.where(strict, big[:2 * L, 2 * L:], 0.0)
        q["rbk"] = _split(jnp.concatenate([jnp.where(incl, big[2 * L:, :2 * L], 0.0),
                                           jnp.where(incl, big[2 * L:, 2 * L:], 0.0)], axis=1))
        q["t"] = eye + q["pw"]
    for lvl in range(int(math.log2(L)) - 1):
        for q in pre:
            sp_pw = _split(q["pw"])
            if lvl > 0:
                q["t"] = q["t"] + _mm3(_split(q["t"]), sp_pw)
            q["pw"] = _mm3(sp_pw, sp_pw)
    for q in pre:
        q["t"] = q["t"] + _mm3(_split(q["t"]), _split(q.pop("pw")))
        q["tmp"] = _mm3(_split(q.pop("a_ak")), _split(q["v"]))
    for q in pre:
        wu = _mm3(_split(q.pop("t")), _split(jnp.concatenate([q.pop("s_a"), q.pop("tmp")], axis=1)))
        q["w"] = _split(wu[:, :2 * L])
        q["u0"] = wu[:, 2 * L:]

    hts = [ht_ref[p] for p in range(n_pairs)]
    for q, (c, p) in zip(pre, items):
        sp_h = _split(hts[p])
        u_s = _mm3(q["w"], sp_h, _NT) + q["u0"]
        sp_uv = _split(jnp.concatenate([u_s, q["v"]], axis=0))
        y_s = _mm3(q["r"], sp_h, _NT) + _mm3(q["rbk"], sp_uv)
        hts[p] = (hts[p] + _mm3(sp_uv, q["bk"], _TN)) * q["p_last"]
        y_ref[c * L:(c + 1) * L, p * LANES:(p + 1) * LANES] = y_s[:L] + y_s[L:]
    for p in range(n_pairs):
        ht_ref[p] = hts[p]

    ones_bd = _head_ones().astype(BF16)
    inv_n = 1.0 / HEAD_DIM
    for p in range(n_pairs):
        cs = slice(p * LANES, (p + 1) * LANES)
        y = y_ref[:, cs]
        mean = _dot_ones(y, ones_bd) * inv_n
        yc = y - mean
        var = _dot_ones(yc * yc, ones_bd) * inv_n
        yn = yc * lax.rsqrt(var + GN_EPS) * lnw_ref[:, cs] + lnb_ref[:, cs]
        r, k, v = r_ref[:, cs], k_ref[:, cs], v_ref[:, cs]
        bonus = _dot_ones(r * k * rk_ref[:, cs], ones_bd)
        o_ref[:, cs] = ((yn + bonus * v) * g_ref[:, cs]).astype(o_ref.dtype)


def rwkv_chunk(r, k, v, kk, b, cum, g, r_k, ln_w, ln_b, n_chunks=4, n_pairs=2):
    S, C = r.shape
    tm = n_chunks * CHUNK
    tw = n_pairs * LANES
    blk = pl.BlockSpec((tm, tw), lambda h, c: (c, h))
    vec = pl.BlockSpec((1, tw), lambda h, c: (0, h))
    return pl.pallas_call(
        functools.partial(_rwkv_chunk_kernel, n_chunks=n_chunks, n_pairs=n_pairs),
        out_shape=jax.ShapeDtypeStruct((S, C), BF16),
        grid=(C // tw, S // tm),
        in_specs=[blk] * 7 + [vec] * 3,
        out_specs=blk,
        scratch_shapes=[pltpu.VMEM((n_pairs, LANES, LANES), F32), pltpu.VMEM((tm, tw), F32)],
        compiler_params=_cparams(("parallel", "arbitrary")),
        name="rwkv_chunk",
    )(r, k, v, kk, b, cum, g, r_k, ln_w, ln_b)


def _gelu_tanh(y):
    return 0.5 * y * (1.0 + jnp.tanh(math.sqrt(2.0 / math.pi) * (y + 0.044715 * (y * y * y))))


def _split3(x):
    hi = x.astype(BF16)
    r1 = x - hi.astype(F32)
    mid = r1.astype(BF16)
    return hi, mid, (r1 - mid.astype(F32)).astype(BF16)


def _cmul(ar, ai, br, bi):
    return ar * br - ai * bi, ar * bi + ai * br


def _s5_kernel(u_ref, bm_ref, cm_ref, kall_ref, ztab_ref, ptab_ref, pw_ref, ct_ref, d_ref, o_ref, carry_ref,
               *, tm):
    half = 8 * S5_STATE
    Q = S5_LAG
    nq = tm // Q

    @pl.when(pl.program_id(1) == 0)
    def _():
        carry_ref[...] = jnp.zeros_like(carry_ref)

    u = u_ref[...]
    pos = lax.broadcasted_iota(jnp.int32, (tm, 1), 0) % Q
    lags = [u] + [jnp.where(pos >= tau, pltpu.roll(u, tau, axis=0), 0.0) for tau in range(1, Q)]
    y = _mm3(_split(jnp.concatenate(lags, axis=1)), (kall_ref[0, 0], kall_ref[0, 1]))

    bu = _mm3(_split(u), (bm_ref[0, 0], bm_ref[0, 1]))
    wr, wi = _cmul(bu[:, :half], bu[:, half:], ztab_ref[0, :, :half], ztab_ref[0, :, half:])
    seg = jnp.where(lax.broadcasted_iota(jnp.int32, (nq, tm), 1) // Q
                    == lax.broadcasted_iota(jnp.int32, (nq, tm), 0), 1.0, 0.0).astype(BF16)
    z = sum(jnp.dot(seg, part, preferred_element_type=F32) for part in _split3(jnp.concatenate([wr, wi], axis=1)))

    xr, xi = z[:, :half], z[:, half:]
    crow = lax.broadcasted_iota(jnp.int32, (nq, 1), 0)
    for lvl in range(int(math.log2(nq))):
        off = 1 << lvl
        ar, ai = pw_ref[0, lvl:lvl + 1, :half], pw_ref[0, lvl:lvl + 1, half:]
        keep = crow >= off
        sr = jnp.where(keep, pltpu.roll(xr, off, axis=0), 0.0)
        si = jnp.where(keep, pltpu.roll(xi, off, axis=0), 0.0)
        xr, xi = xr + ar * sr - ai * si, xi + ar * si + ai * sr
    cr, ci = carry_ref[:, :half], carry_ref[:, half:]
    tr, ti = _cmul(ct_ref[0, :, :half], ct_ref[0, :, half:], cr, ci)
    xr, xi = xr + tr, xi + ti
    er = jnp.where(crow == 0, cr, pltpu.roll(xr, 1, axis=0))
    ei = jnp.where(crow == 0, ci, pltpu.roll(xi, 1, axis=0))
    carry_ref[:, :half] = xr[nq - 1:nq, :]
    carry_ref[:, half:] = xi[nq - 1:nq, :]

    expand = jnp.where(lax.broadcasted_iota(jnp.int32, (tm, nq), 0) // Q
                       == lax.broadcasted_iota(jnp.int32, (tm, nq), 1), 1.0, 0.0).astype(BF16)
    pe = sum(jnp.dot(expand, part, preferred_element_type=F32)
             for part in _split3(jnp.concatenate([er, ei], axis=1)))
    sr, si = _cmul(pe[:, :half], pe[:, half:], ptab_ref[0, :, :half], ptab_ref[0, :, half:])
    y = y + _mm3(_split(jnp.concatenate([sr, si], axis=1)), (cm_ref[0, 0], cm_ref[0, 1])) + d_ref[...] * u
    o_ref[...] = _gelu_tanh(y)


def s5_scan(p, col0, tables, d, tm):
    bm, cm, kall, ztab, ptab, pows, ctab = tables
    S = p.shape[0]
    nj = S5_WIDTH // LANES
    nst = 2 * 8 * S5_STATE
    nq = tm // S5_LAG
    per_j = lambda *shape: pl.BlockSpec((1,) + shape, lambda j, i: (j,) + (0,) * len(shape))
    return pl.pallas_call(
        functools.partial(_s5_kernel, tm=tm),
        out_shape=jax.ShapeDtypeStruct((S, S5_WIDTH), F32),
        grid=(nj, S // tm),
        in_specs=[pl.BlockSpec((tm, LANES), lambda j, i: (i, col0 // LANES + j)),
                  per_j(2, LANES, nst), per_j(2, nst, LANES), per_j(2, S5_LAG * LANES, LANES),
                  per_j(tm, nst), per_j(tm, nst), per_j(pows.shape[1], nst), per_j(nq, nst),
                  pl.BlockSpec((1, LANES), lambda j, i: (0, j))],
        out_specs=pl.BlockSpec((tm, LANES), lambda j, i: (i, j)),
        scratch_shapes=[pltpu.VMEM((1, nst), F32)],
        compiler_params=_cparams(("parallel", "arbitrary")),
        name="s5_scan",
    )(p, bm, cm, kall, ztab, ptab, pows, ctab, d)


def _s5_tables(lam_re, lam_im, log_step, b_re, b_im, c_re, c_im, tm):
    G, P, GS = S5_WIDTH // S5_GROUP, S5_STATE, S5_GROUP
    nj = G // 8
    Q = S5_LAG
    nq = tm // Q
    lam = lax.complex(jnp.minimum(lam_re, -1e-4), lam_im)
    step = jnp.exp(log_step)[:, None]
    lam_dt = lam * step
    lam_bar = jnp.exp(lam_dt)
    b_bar = ((lam_bar - 1.0) / lam)[..., None] * lax.complex(b_re, b_im)

    def lam_pow(e):
        return jnp.exp(lam_dt[None] * e.astype(F32)[:, None, None])

    def pack(z):
        z = jnp.moveaxis(z.reshape(z.shape[0], nj, 8 * P), 1, 0)
        return jnp.concatenate([jnp.real(z), jnp.imag(z)], axis=-1).astype(F32)

    def split(x):
        hi = x.astype(BF16)
        return jnp.stack([hi, (x - hi.astype(F32)).astype(BF16)], axis=1)

    t = jnp.arange(tm)
    ztab = pack(lam_pow(Q - 1 - t % Q))
    ptab = pack(lam_pow(t % Q + 1))
    pows = pack(lam_pow(Q * 2 ** jnp.arange(int(math.log2(nq)))))
    ctab = pack(lam_pow(Q * (jnp.arange(nq) + 1)))

    eye8 = jnp.eye(8, dtype=F32)
    bb = b_bar.reshape(nj, 8, P, GS)
    def bd_in(x):
        return jnp.einsum('jgph,gk->jghkp', x, eye8).reshape(nj, 8 * GS, 8 * P)
    bmat = jnp.concatenate([bd_in(jnp.real(bb)), bd_in(jnp.imag(bb))], axis=-1).astype(F32)
    def bd_out(x):
        return jnp.einsum('jghp,gk->jkpgh', x, eye8).reshape(nj, 8 * P, 8 * GS)
    cmat = jnp.concatenate([bd_out(c_re.reshape(nj, 8, GS, P)), -bd_out(c_im.reshape(nj, 8, GS, P))],
                           axis=1).astype(F32)
    m = lam_pow(jnp.arange(Q))[..., None] * b_bar[None]
    ktab = (jnp.einsum('ghp,tgpk->tghk', c_re, jnp.real(m), precision=HI)
            - jnp.einsum('ghp,tgpk->tghk', c_im, jnp.imag(m), precision=HI))
    kall = jnp.einsum('tjghk,gm->jtgkmh', ktab.reshape(Q, nj, 8, GS, GS), eye8).reshape(nj, Q * 8 * GS, 8 * GS)
    return split(bmat), split(cmat), split(kall.astype(F32)), ztab, ptab, pows, ctab


def _glu_kernel(y_ref, yc_ref, w_ref, b_ref, o_ref, a_ref):
    @pl.when(pl.program_id(1) == 0)
    def _():
        a_ref[...] = y_ref[...].astype(BF16)

    z = _dot(a_ref[...], w_ref[...]) + b_ref[...]
    o_ref[...] = (yc_ref[...] * _sigmoid(z)).astype(o_ref.dtype)


def glu(y, w, b, tm=512, tn=512):
    S, C = y.shape
    return pl.pallas_call(
        _glu_kernel,
        out_shape=jax.ShapeDtypeStruct((S, C), BF16),
        grid=(S // tm, C // tn),
        in_specs=[pl.BlockSpec((tm, C), lambda i, j: (i, 0)), pl.BlockSpec((tm, tn), lambda i, j: (i, j)),
                  pl.BlockSpec((C, tn), lambda i, j: (0, j)), pl.BlockSpec((1, tn), lambda i, j: (0, j))],
        out_specs=pl.BlockSpec((tm, tn), lambda i, j: (i, j)),
        scratch_shapes=[pltpu.VMEM((tm, C), BF16)],
        compiler_params=_cparams(("parallel", "arbitrary")),
        name="s5_glu",
    )(y, y, w, b)


def _attn_kernel(q_ref, kc_ref, kp_ref, vc_ref, vp_ref, o_ref, lse_ref, *, tq):
    B = ATTN_BLK
    first_tile = pl.program_id(1) == 0
    q = q_ref[...] * (HEAD_DIM ** -0.5)
    kf = jnp.concatenate([kp_ref[...], kc_ref[...]], axis=0)
    vf = jnp.concatenate([vp_ref[...], vc_ref[...]], axis=0)
    lane = lax.broadcasted_iota(jnp.int32, (B, LANES), 1)
    head0 = lane < HEAD_DIM
    qi = lax.broadcasted_iota(jnp.int32, (B, 2 * B), 0)
    kj = lax.broadcasted_iota(jnp.int32, (B, 2 * B), 1)
    band = (kj >= qi) & (kj <= qi + B)
    zero = jnp.zeros((), q.dtype)
    for qb in range(tq // B):
        qs = q[qb * B:(qb + 1) * B]
        ks = kf[qb * B:(qb + 2) * B]
        vs = vf[qb * B:(qb + 2) * B]
        mask = band
        if qb == 0:
            mask = band & (kj >= jnp.where(first_tile, B, 0))
        outs, lses = [], []
        for hmask in (head0, jnp.logical_not(head0)):
            s = _dot_nt(jnp.where(hmask, qs, zero), ks)
            s = jnp.where(mask, s, -jnp.inf)
            m = jnp.max(s, axis=-1, keepdims=True)
            pexp = jnp.exp(s - m)
            den = jnp.sum(pexp, axis=-1, keepdims=True)
            outs.append(_dot(pexp.astype(vs.dtype), vs) / den)
            lses.append(m + jnp.log(den))
        o_ref[qb * B:(qb + 1) * B, :] = jnp.where(head0, outs[0], outs[1])
        lse_ref[qb * B:(qb + 1) * B, :] = jnp.where(head0, lses[0], lses[1])


def attn_branch(qkv, tq):
    d, Sd, _ = qkv.shape
    tq = min(tq, Sd)
    nh = D_MODEL // LANES
    rpb = tq // ATTN_BLK

    def cur(part):
        return pl.BlockSpec((None, tq, LANES), lambda c, i: (c // nh, i, part * nh + c % nh))

    def prev(part):
        return pl.BlockSpec((None, ATTN_BLK, LANES),
                            lambda c, i: (c // nh, jnp.maximum(i * rpb - 1, 0), part * nh + c % nh))

    out = jax.ShapeDtypeStruct((d, Sd, D_MODEL), F32)
    return pl.pallas_call(
        functools.partial(_attn_kernel, tq=tq),
        out_shape=[out, out],
        grid=(d * nh, Sd // tq),
        in_specs=[cur(0), cur(1), prev(1), cur(2), prev(2)],
        out_specs=[pl.BlockSpec((None, tq, LANES), lambda c, i: (c // nh, i, c % nh))] * 2,
        compiler_params=_cparams(("parallel", "arbitrary")),
        name="dilated_attn_d%d" % d,
    )(qkv, qkv, qkv, qkv, qkv)


def _merge_kernel(*refs, dilations):
    nb = len(dilations)
    o_refs, l_refs = refs[:nb], refs[nb:2 * nb]
    out_ref = refs[2 * nb]
    scratches = refs[2 * nb + 1:]
    tm, tn = out_ref.shape

    def natural(ref, d, scratch, c):
        cs = slice(c * LANES, (c + 1) * LANES)
        if d == 1:
            return ref[0, :, cs]
        for r in range(d):
            scratch[c, pl.ds(r, tm // d, stride=d), :] = ref[r, :, cs]
        return scratch[c]

    for c in range(tn // LANES):
        os_ = [natural(o, d, s, c) for o, d, s in zip(o_refs, dilations, scratches[:nb])]
        ls_ = [natural(l, d, s, c) for l, d, s in zip(l_refs, dilations, scratches[nb:])]
        m = functools.reduce(jnp.maximum, ls_)
        es = [jnp.exp(l - m) for l in ls_]
        num = functools.reduce(lambda a, b: a + b, [e * o for e, o in zip(es, os_)])
        den = functools.reduce(lambda a, b: a + b, es)
        out_ref[:, c * LANES:(c + 1) * LANES] = (num / den).astype(out_ref.dtype)


def merge_branches(outs, lses, tm=512, tn=512):
    dilations = tuple(o.shape[0] for o in outs)
    S = outs[0].shape[0] * outs[0].shape[1]
    D = outs[0].shape[2]
    in_specs = [pl.BlockSpec((d, tm // d, tn), lambda i, j: (0, i, j)) for d in dilations] * 2
    return pl.pallas_call(
        functools.partial(_merge_kernel, dilations=dilations),
        out_shape=jax.ShapeDtypeStruct((S, D), BF16),
        grid=(S // tm, D // tn),
        in_specs=in_specs,
        out_specs=pl.BlockSpec((tm, tn), lambda i, j: (i, j)),
        scratch_shapes=[pltpu.VMEM((tn // LANES, tm, LANES), F32)] * (2 * len(dilations)),
        compiler_params=_cparams(("parallel", "parallel")),
        name="attn_merge",
    )(*outs, *lses)


def _rotary_tables(S):
    half = ROT_DIM // 2
    inv = ROPE_THETA ** (-jnp.arange(half, dtype=F32) * 2.0 / ROT_DIM)
    ang = jnp.arange(S, dtype=F32)[:, None] * inv[None, :]
    cos, sin = jnp.cos(ang), jnp.sin(ang)
    rest = HEAD_DIM - ROT_DIM
    ones = jnp.ones((S, rest), F32)
    zeros = jnp.zeros((S, rest), F32)
    zh = jnp.zeros((S, half), F32)
    cos_t = jnp.concatenate([cos, cos, ones], axis=1)
    sin_a = jnp.concatenate([-sin, zh, zeros], axis=1)
    sin_b = jnp.concatenate([zh, sin, zeros], axis=1)
    rep = LANES // HEAD_DIM
    return tuple(jnp.tile(t, (1, rep)) for t in (cos_t, sin_a, sin_b))


def _pad_rows(w, n):
    return jnp.pad(w, ((0, n - w.shape[0]), (0, 0)))


def kernel(x, c, ada_w, ada_b, norm_mix_g, norm_ffn_g, hyb_w_in, hyb_w_out, rwkv_mu, rwkv_w0, rwkv_w2, rwkv_a0, rwkv_a2, rwkv_g2, rwkv_k_k, rwkv_k_a, rwkv_r_k, rwkv_ln_w, rwkv_ln_b, rwkv_v0, rwkv_v1, rwkv_v2, s5_lam_re, s5_lam_im, s5_log_step, s5_b_re, s5_b_im, s5_c_re, s5_c_im, s5_d, s5_glu_w, s5_glu_b, attn_w_qkv, attn_w_o, ffn_w_in, ffn_w_out, final_norm_g):
    B, S, D = x.shape
    assert B == 1 and D == D_MODEL and S % 2048 == 0
    C = RWKV_WIDTH
    xs = x.reshape(S, D)
    mod = ada_modulation(c, ada_w, ada_b)
    rot_tables = _rotary_tables(S)
    s5_tm = 256
    v_first = None
    row = lambda t: t.reshape(1, -1)

    for i in range(DEPTH):
        m = mod[i]
        sh_mix, sc_mix, gt_mix, sh_ffn, sc_ffn, gt_ffn = [m[:, q * D:(q + 1) * D] for q in range(6)]
        if i % 2 == 0:
            j = i // 2
            w_in = hyb_w_in[j]
            gpad = RWKV_IN_PAD - RWKV_IN
            w_in = jnp.concatenate([w_in[:, :RWKV_IN], jnp.zeros((D, gpad), F32), w_in[:, RWKV_IN:]],
                                   axis=1).astype(BF16)
            mu = jnp.concatenate([rwkv_mu[j], jnp.zeros((gpad,), F32)]).reshape(1, RWKV_IN_PAD)
            p = nm_matmul(xs, row(norm_mix_g[i]), sh_mix, sc_mix, w_in, F32, tm=512, tn=640)

            w2p = jnp.concatenate([rwkv_w2[j], jnp.zeros((LORA_A, C), F32)], axis=0)
            a2p = jnp.concatenate([jnp.zeros((LORA_W, C), F32), rwkv_a2[j]], axis=0)
            g2p = _pad_rows(rwkv_g2[j], 2 * LANES)
            vres = None
            if j > 0:
                v1p = jnp.pad(rwkv_v1[j - 1], ((0, 0), (0, LANES - LORA_V)))
                v2p = _pad_rows(rwkv_v2[j - 1], LANES)
                vres = (v_first, row(rwkv_v0[j - 1]), v1p, v2p)
            r, k, v, kk, b, cum, g = rwkv_prep(p, mu, row(rwkv_w0[j]), row(rwkv_a0[j]), row(rwkv_k_k[j]),
                                               row(rwkv_k_a[j]), w2p, a2p, g2p, vres)
            if j == 0:
                v_first = v
            y_rwkv = rwkv_chunk(r, k, v, kk, b, cum, g, row(rwkv_r_k[j]), row(rwkv_ln_w[j]),
                                row(rwkv_ln_b[j]))

            s5_tabs = _s5_tables(s5_lam_re[j], s5_lam_im[j], s5_log_step[j], s5_b_re[j], s5_b_im[j],
                                 s5_c_re[j], s5_c_im[j], s5_tm)
            y_s5 = s5_scan(p, RWKV_IN_PAD, s5_tabs, row(s5_d[j]), tm=s5_tm)
            y_s5 = glu(y_s5, s5_glu_w[j].astype(BF16), row(s5_glu_b[j]))
            xs = proj_residual([y_rwkv, y_s5], hyb_w_out[j].astype(BF16), xs, gt_mix, tm=512, tn=512)
        else:
            j = i // 2
            dilations = tuple(d for (_, d) in DILATED_BRANCHES)
            qkvs = qkv_proj(xs, row(norm_mix_g[i]), sh_mix, sc_mix, attn_w_qkv[j].astype(BF16), rot_tables,
                            2 * D, dilations, tm=512, tn=512)
            outs, lses = zip(*[attn_branch(t, tq=512) for t in qkvs])
            o = merge_branches(outs, lses)
            xs = proj_residual([o], attn_w_o[j].astype(BF16), xs, gt_mix, tm=512, tn=512)
        xs = ffn(xs, row(norm_ffn_g[i]), sh_ffn, sc_ffn, gt_ffn, ffn_w_in[i].astype(BF16),
                 ffn_w_out[i].astype(BF16), row(final_norm_g), final_norm=(i == DEPTH - 1))
    return xs.reshape(B, S, D)
```

```python
import functools
import math

import jax
import jax.numpy as jnp
from jax import lax
from jax.experimental import pallas as pl
from jax.experimental.pallas import tpu as pltpu

F32 = jnp.float32
BF16 = jnp.bfloat16
HI = lax.Precision.HIGHEST

D_MODEL = 2048
DEPTH = 4
HEAD_DIM = 64
RWKV_WIDTH = 1024
LORA_W = 64
LORA_A = 64
LORA_V = 32
LORA_G = 160
RWKV_IN = 3 * RWKV_WIDTH + LORA_W + LORA_A + LORA_G
RWKV_IN_PAD = 3456
S5_WIDTH = 1024
S5_GROUP = 16
S5_STATE = 64
S5_LAG = 8
EVEN_IN_PAD = RWKV_IN_PAD + S5_WIDTH
ATTN_HEADS = 32
ROT_DIM = 16
ROPE_THETA = 500000.0
DILATED_BRANCHES = ((128, 1), (512, 4), (2048, 16))
ATTN_BLK = 128
ATTN_TILE = 2048
FFN_HIDDEN = 5632
RMS_EPS = 1e-6
GN_EPS = 64e-5

LANES = 128
CHUNK = 64
VMEM_LIMIT = 48 * 1024 * 1024


def _cparams(sem):
    return pltpu.CompilerParams(dimension_semantics=sem, vmem_limit_bytes=VMEM_LIMIT)


def _sigmoid(x):
    return 1.0 / (1.0 + jnp.exp(-x))


def _dot(a, b, prec=None):
    return jnp.dot(a, b, preferred_element_type=F32, precision=prec)


def _dot_nt(a, b, prec=None):
    return lax.dot_general(a, b, (((1,), (1,)), ((), ())), preferred_element_type=F32, precision=prec)


def _dot_tn(a, b, prec=None):
    return lax.dot_general(a, b, (((0,), (0,)), ((), ())), preferred_element_type=F32, precision=prec)


def _norm_mod(x, g, shift, scale):
    ms = jnp.mean(x * x, axis=-1, keepdims=True)
    return (x * lax.rsqrt(ms + RMS_EPS) * g) * (1.0 + scale) + shift


def _ada_kernel(c_ref, w_ref, b_ref, o_ref):
    c = c_ref[...]
    s = c * _sigmoid(c)
    o_ref[0] = jnp.sum(s * w_ref[0], axis=0, keepdims=True) + b_ref[0]


def ada_modulation(c, ada_w, ada_b, tn=1024):
    L, D, N = ada_w.shape
    return pl.pallas_call(
        _ada_kernel,
        out_shape=jax.ShapeDtypeStruct((L, 1, N), F32),
        grid=(L, N // tn),
        in_specs=[pl.BlockSpec((D, 1), lambda l, j: (0, 0)),
                  pl.BlockSpec((1, D, tn), lambda l, j: (l, 0, j)),
                  pl.BlockSpec((1, 1, tn), lambda l, j: (l, 0, j))],
        out_specs=pl.BlockSpec((1, 1, tn), lambda l, j: (l, 0, j)),
        compiler_params=_cparams(("parallel", "parallel")),
        name="ada_modulation",
    )(c.reshape(D, 1), ada_w, ada_b.reshape(L, 1, N))


def _nm_matmul_kernel(x_ref, g_ref, sh_ref, sc_ref, w_ref, o_ref, h_ref):
    @pl.when(pl.program_id(1) == 0)
    def _():
        h_ref[...] = _norm_mod(x_ref[...], g_ref[...], sh_ref[...], sc_ref[...]).astype(BF16)

    o_ref[...] = _dot(h_ref[...], w_ref[...]).astype(o_ref.dtype)


def _nm_specs(x, w, tm, tn):
    D = x.shape[1]
    row = lambda i, j: (i, 0)
    vec = lambda i, j: (0, 0)
    return [pl.BlockSpec((tm, D), row), pl.BlockSpec((1, D), vec), pl.BlockSpec((1, D), vec),
            pl.BlockSpec((1, D), vec), pl.BlockSpec((D, tn), lambda i, j: (0, j))]


def nm_matmul(x, g, shift, scale, w, out_dtype, tm, tn):
    S, D = x.shape
    N = w.shape[1]
    return pl.pallas_call(
        _nm_matmul_kernel,
        out_shape=jax.ShapeDtypeStruct((S, N), out_dtype),
        grid=(S // tm, N // tn),
        in_specs=_nm_specs(x, w, tm, tn),
        out_specs=pl.BlockSpec((tm, tn), lambda i, j: (i, j)),
        scratch_shapes=[pltpu.VMEM((tm, D), BF16)],
        compiler_params=_cparams(("parallel", "arbitrary")),
        name="norm_mod_matmul",
    )(x, g, shift, scale, w)


def _qkv_kernel(x_ref, g_ref, sh_ref, sc_ref, w_ref, cos_ref, sa_ref, sb_ref, *rest, n_rot_blocks, dilations):
    o_refs, (h_ref, t_ref) = rest[:len(dilations)], rest[len(dilations):]
    j = pl.program_id(1)
    n_lane_blocks, tm, _ = t_ref.shape

    @pl.when(j == 0)
    def _():
        h_ref[...] = _norm_mod(x_ref[...], g_ref[...], sh_ref[...], sc_ref[...]).astype(BF16)

    acc = _dot(h_ref[...], w_ref[...])

    @pl.when(j < n_rot_blocks)
    def _():
        cos, sa, sb = cos_ref[...], sa_ref[...], sb_ref[...]
        half = ROT_DIM // 2
        for c in range(n_lane_blocks):
            t = acc[:, c * LANES:(c + 1) * LANES]
            t_ref[c] = t * cos + pltpu.roll(t, LANES - half, axis=1) * sa + pltpu.roll(t, half, axis=1) * sb

    @pl.when(j >= n_rot_blocks)
    def _():
        for c in range(n_lane_blocks):
            t_ref[c] = acc[:, c * LANES:(c + 1) * LANES]

    for d, o_ref in zip(dilations, o_refs):
        for c in range(n_lane_blocks):
            cs = slice(c * LANES, (c + 1) * LANES)
            if d == 1:
                o_ref[0, :, cs] = t_ref[c].astype(o_ref.dtype)
                continue
            for r in range(d):
                o_ref[r, :, cs] = t_ref[c, pl.ds(r, tm // d, stride=d), :].astype(o_ref.dtype)


def qkv_proj(x, g, shift, scale, w, rot_tables, n_rot_cols, dilations, tm, tn):
    S, D = x.shape
    N = w.shape[1]
    in_specs = _nm_specs(x, w, tm, tn) + [pl.BlockSpec((tm, LANES), lambda i, j: (i, 0))] * 3
    return pl.pallas_call(
        functools.partial(_qkv_kernel, n_rot_blocks=n_rot_cols // tn, dilations=dilations),
        out_shape=[jax.ShapeDtypeStruct((d, S // d, N), BF16) for d in dilations],
        grid=(S // tm, N // tn),
        in_specs=in_specs,
        out_specs=[pl.BlockSpec((d, tm // d, tn), lambda i, j: (0, i, j)) for d in dilations],
        scratch_shapes=[pltpu.VMEM((tm, D), BF16), pltpu.VMEM((tn // LANES, tm, LANES), F32)],
        compiler_params=_cparams(("parallel", "arbitrary")),
        name="qkv_proj",
    )(x, g, shift, scale, w, *rot_tables)


def _proj_res_kernel(*refs, n_a):
    a_refs, w_refs = refs[:n_a], refs[n_a:2 * n_a]
    x_ref, gt_ref, o_ref = refs[2 * n_a:]
    acc = _dot(a_refs[0][...], w_refs[0][...])
    for a_ref, w_ref in zip(a_refs[1:], w_refs[1:]):
        acc = acc + _dot(a_ref[...], w_ref[...])
    o_ref[...] = x_ref[...] + gt_ref[...] * acc


def proj_residual(a_list, w, x, gate, tm, tn):
    S, N = x.shape
    n_a = len(a_list)
    in_specs, args = [], []
    for a in a_list:
        in_specs.append(pl.BlockSpec((tm, a.shape[1]), lambda i, j: (i, 0)))
        args.append(a)
    off = 0
    for a in a_list:
        ka = a.shape[1]
        in_specs.append(pl.BlockSpec((ka, tn), functools.partial(lambda i, j, b: (b, j), b=off // ka)))
        args.append(w)
        off += ka
    in_specs += [pl.BlockSpec((tm, tn), lambda i, j: (i, j)), pl.BlockSpec((1, tn), lambda i, j: (0, j))]
    args += [x, gate]
    return pl.pallas_call(
        functools.partial(_proj_res_kernel, n_a=n_a),
        out_shape=jax.ShapeDtypeStruct((S, N), F32),
        grid=(S // tm, N // tn),
        in_specs=in_specs,
        out_specs=pl.BlockSpec((tm, tn), lambda i, j: (i, j)),
        compiler_params=_cparams(("parallel", "parallel")),
        name="proj_residual",
    )(*args)


def _ffn_kernel(x_ref, g_ref, sh_ref, sc_ref, gt_ref, wg_ref, wu_ref, wo_ref, fg_ref, o_ref, h_ref, acc_ref,
                *, final_norm):
    k = pl.program_id(1)

    @pl.when(k == 0)
    def _():
        h_ref[...] = _norm_mod(x_ref[...], g_ref[...], sh_ref[...], sc_ref[...]).astype(BF16)
        acc_ref[...] = jnp.zeros_like(acc_ref)

    h = h_ref[...]
    gate = _dot(h, wg_ref[...])
    up = _dot(h, wu_ref[...])
    act = (gate * _sigmoid(gate) * up).astype(BF16)
    acc_ref[...] += _dot(act, wo_ref[...])

    @pl.when(k == pl.num_programs(1) - 1)
    def _():
        xn = x_ref[...] + gt_ref[...] * acc_ref[...]
        if final_norm:
            ms = jnp.mean(xn * xn, axis=-1, keepdims=True)
            xn = xn * lax.rsqrt(ms + RMS_EPS) * fg_ref[...]
        o_ref[...] = xn


def ffn(x, g, shift, scale, gate, w_in, w_out, final_g, final_norm, tm=512, th=512):
    S, D = x.shape
    H = w_out.shape[0]
    nk = H // th
    row = lambda i, k: (i, 0)
    vec = lambda i, k: (0, 0)
    return pl.pallas_call(
        functools.partial(_ffn_kernel, final_norm=final_norm),
        out_shape=jax.ShapeDtypeStruct((S, D), F32),
        grid=(S // tm, nk),
        in_specs=[pl.BlockSpec((tm, D), row), pl.BlockSpec((1, D), vec), pl.BlockSpec((1, D), vec),
                  pl.BlockSpec((1, D), vec), pl.BlockSpec((1, D), vec),
                  pl.BlockSpec((D, th), lambda i, k: (0, k)),
                  pl.BlockSpec((D, th), lambda i, k: (0, k + nk)),
                  pl.BlockSpec((th, D), lambda i, k: (k, 0)),
                  pl.BlockSpec((1, D), vec)],
        out_specs=pl.BlockSpec((tm, D), row),
        scratch_shapes=[pltpu.VMEM((tm, D), BF16), pltpu.VMEM((tm, D), F32)],
        compiler_params=_cparams(("parallel", "arbitrary")),
        name="swiglu_ffn",
    )(x, g, shift, scale, gate, w_in, w_in, w_out, final_g)


def _segsum_heads(x, ones_bd):
    cols = [_dot_ones(x[:, c * LANES:(c + 1) * LANES], ones_bd) for c in range(x.shape[1] // LANES)]
    return jnp.concatenate(cols, axis=1)


def _head_ones():
    r = lax.broadcasted_iota(jnp.int32, (LANES, LANES), 0) // HEAD_DIM
    c = lax.broadcasted_iota(jnp.int32, (LANES, LANES), 1) // HEAD_DIM
    return jnp.where(r == c, 1.0, 0.0).astype(F32)


def _rwkv_prep_kernel(*refs, has_vres, tm):
    (p_ref, pp_ref, mu_ref, w0_ref, a0_ref, kk_w_ref, ka_ref, w2_ref, a2_ref, g2_ref) = refs[:10]
    if has_vres:
        vf_ref, v0_ref, v1_ref, v2_ref = refs[10:14]
        outs = refs[14:]
    else:
        outs = refs[10:]
    r_o, k_o, v_o, kk_o, b_o, cum_o, g_o = outs
    first = pl.program_id(0) == 0
    row = lax.broadcasted_iota(jnp.int32, (tm, 1), 0)

    def mixed(c0, c1):
        p = p_ref[:, c0:c1]
        prev_last = jnp.where(first, 0.0, pp_ref[7:8, c0:c1])
        sh = jnp.where(row == 0, prev_last, pltpu.roll(p, 1, axis=0))
        return p + (sh - p) * mu_ref[:, c0:c1]

    C = RWKV_WIDTH
    r = mixed(0, C)
    k = mixed(C, 2 * C)
    v = mixed(2 * C, 3 * C)
    wa = mixed(3 * C, 3 * C + LANES)
    gl = mixed(3 * C + LANES, RWKV_IN_PAD)

    lora = lambda act, w_ref: _mm3(_split(act), (w_ref[0], w_ref[1]))
    wlog = w0_ref[...] + lora(jnp.tanh(wa), w2_ref)
    xs = -wlog
    softplus = jnp.maximum(xs, 0.0) + jnp.log(1.0 + jnp.exp(-jnp.abs(xs)))
    w = -softplus - 0.5
    log_decay = -jnp.exp(w)
    a = _sigmoid(a0_ref[...] + lora(wa, a2_ref))
    g = lora(_sigmoid(gl), g2_ref)
    if has_vres:
        v = v + (vf_ref[...] - v) * _sigmoid(v0_ref[...] + lora(lora(v, v1_ref), v2_ref))

    kk = k * kk_w_ref[...]
    nrm = jnp.sqrt(_segsum_heads(kk * kk, _head_ones().astype(BF16)))
    kk = kk / jnp.maximum(nrm, 1e-12)
    k = k * (1.0 + (a - 1.0) * ka_ref[...])

    ti = lax.broadcasted_iota(jnp.int32, (tm, tm), 0)
    si = lax.broadcasted_iota(jnp.int32, (tm, tm), 1)
    tri = jnp.where((si <= ti) & (si // CHUNK == ti // CHUNK), 1.0, 0.0).astype(BF16)
    cum = sum(jnp.dot(tri, part, preferred_element_type=F32) for part in _split3(log_decay))

    r_o[...] = r
    k_o[...] = k
    v_o[...] = v
    kk_o[...] = kk
    b_o[...] = kk * a
    cum_o[...] = cum
    g_o[...] = g


def rwkv_prep(p, mu, w0, a0, k_k, k_a, w2p, a2p, g2p, vres, tm=256):
    S = p.shape[0]
    C = RWKV_WIDTH
    row = lambda i: (i, 0)
    vec = lambda i: (0, 0)
    whole = lambda w: pl.BlockSpec(w.shape, lambda i: (0,) * w.ndim)
    in_specs = [pl.BlockSpec((tm, RWKV_IN_PAD), row),
                pl.BlockSpec((8, RWKV_IN_PAD), lambda i: (jnp.maximum(i * (tm // 8) - 1, 0), 0)),
                pl.BlockSpec((1, RWKV_IN_PAD), vec),
                pl.BlockSpec((1, C), vec), pl.BlockSpec((1, C), vec), pl.BlockSpec((1, C), vec),
                pl.BlockSpec((1, C), vec), whole(w2p), whole(a2p), whole(g2p)]
    args = [p, p, mu, w0, a0, k_k, k_a, w2p, a2p, g2p]
    if vres is not None:
        v_first, v0, v1p, v2p = vres
        in_specs += [pl.BlockSpec((tm, C), row), pl.BlockSpec((1, C), vec), whole(v1p), whole(v2p)]
        args += [v_first, v0, v1p, v2p]
    out = jax.ShapeDtypeStruct((S, C), F32)
    return pl.pallas_call(
        functools.partial(_rwkv_prep_kernel, has_vres=vres is not None, tm=tm),
        out_shape=[out] * 7,
        grid=(S // tm,),
        in_specs=in_specs,
        out_specs=[pl.BlockSpec((tm, C), row)] * 7,
        compiler_params=_cparams(("parallel",)),
        name="rwkv_prep",
    )(*args)


_NN = (((1,), (0,)), ((), ()))
_NT = (((1,), (1,)), ((), ()))
_TN = (((0,), (0,)), ((), ()))


def _split(x):
    hi = x.astype(BF16)
    return hi, (x - hi.astype(F32)).astype(BF16)


def _mm3(a, b, dims=_NN):
    f = lambda x, y: lax.dot_general(x, y, dims, preferred_element_type=F32)
    return f(a[0], b[0]) + (f(a[0], b[1]) + f(a[1], b[0]))


def _dot_ones(x, ones_bf16):
    hi = x.astype(BF16)
    r1 = x - hi.astype(F32)
    mid = r1.astype(BF16)
    lo = (r1 - mid.astype(F32)).astype(BF16)
    f = lambda t: jnp.dot(t, ones_bf16, preferred_element_type=F32)
    return f(hi) + (f(mid) + f(lo))


def _rwkv_chunk_kernel(r_ref, k_ref, v_ref, kk_ref, b_ref, cum_ref, g_ref, rk_ref, lnw_ref, lnb_ref,
                       o_ref, ht_ref, y_ref, *, n_chunks, n_pairs):
    L = CHUNK

    @pl.when(pl.program_id(1) == 0)
    def _():
        ht_ref[...] = jnp.zeros_like(ht_ref)

    row = lax.broadcasted_iota(jnp.int32, (L, LANES), 0)
    lane = lax.broadcasted_iota(jnp.int32, (L, LANES), 1)
    head0 = lane < HEAD_DIM
    ri = lax.broadcasted_iota(jnp.int32, (2 * L, 2 * L), 0)
    ci = lax.broadcasted_iota(jnp.int32, (2 * L, 2 * L), 1)
    same = (ri // L) == (ci // L)
    strict = same & ((ci % L) < (ri % L))
    incl = same & ((ci % L) <= (ri % L))

    def stack(x):
        return jnp.concatenate([jnp.where(head0, x, 0.0), jnp.where(head0, 0.0, x)], axis=0)

    items = [(c, p) for c in range(n_chunks) for p in range(n_pairs)]
    pre = [dict() for _ in items]
    for q, (c, p) in zip(pre, items):
        sl = (slice(c * L, (c + 1) * L), slice(p * LANES, (p + 1) * LANES))
        r, k, v, kk, b, cum = r_ref[sl], k_ref[sl], v_ref[sl], kk_ref[sl], b_ref[sl], cum_ref[sl]
        cum_excl = jnp.where(row == 0, 0.0, pltpu.roll(cum, 1, axis=0))
        e_pos = jnp.exp(cum)
        e_neg = jnp.exp(-cum)
        q["s_a"] = stack(-kk * jnp.exp(cum_excl))
        q["v"] = stack(v)
        q["s_r"] = stack(r * e_pos)
        sp_ar = _split(jnp.concatenate([q["s_a"], q["s_r"]], axis=0))
        q["bk"] = _split(jnp.concatenate([stack(b * e_neg), stack(k * e_neg)], axis=0))
        q["p_last"] = e_pos[L - 1:L, :]
        q["big"] = _mm3(sp_ar, q["bk"], _NT)
    for q in pre:
        big = q.pop("big")
        q["pw"] = jnp.where(strict, big[:2 * L, :2 * L], 0.0)
        q["a_ak"] = jnp.where(strict, big[:2 * L, 2 * L:], 0.0)
        q["rbk"] = _split(jnp.concatenate([jnp.where(incl, big[2 * L:, :2 * L], 0.0),
                                           jnp.where(incl, big[2 * L:, 2 * L:], 0.0)], axis=1))
    mm = lambda x, y: _mm3(_split(x), _split(y))
    mmb = lambda x, y: jnp.dot(x.astype(BF16), y.astype(BF16), preferred_element_type=F32)
    m0 = 8
    blk_of = lambda m: ((ri // m) == (ci // m))
    for q in pre:
        d1 = jnp.where(blk_of(m0), q["pw"], 0.0)
        q["d1"] = d1
        q["d2"] = mmb(d1, d1)
    for q in pre:
        d1, d2 = q.pop("d1"), q["d2"]
        q["n"] = d1 + d2 + mmb(d1, d2)
        q["d4"] = mmb(d2, d2)
        q.pop("d2")
    for q in pre:
        d4 = q.pop("d4")
        q["n"] = q["n"] + d4 + mmb(q["n"], d4)
    m = m0
    while m < L:
        for q in pre:
            e = jnp.where(blk_of(2 * m) & jnp.logical_not(blk_of(m)), q["pw"], 0.0)
            q["x"] = e + mm(q["n"], e)
        for q in pre:
            x = q.pop("x")
            q["n"] = q["n"] + x + mm(x, q["n"])
        m *= 2
    for q in pre:
        q.pop("pw")
        q["tmp"] = mm(q.pop("a_ak"), q["v"])
    for q in pre:
        x = jnp.concatenate([q.pop("s_a"), q.pop("tmp")], axis=1)
        q["wu"] = x + mm(q.pop("n"), x)
    for q in pre:
        wu = q.pop("wu")
        sp_w = _split(wu[:, :2 * L])
        sp_uv0 = _split(jnp.concatenate([wu[:, 2 * L:], q.pop("v")], axis=0))
        bk, rbk = q.pop("bk"), q.pop("rbk")
        q["mt"] = _split(_mm3(sp_w, (bk[0][:2 * L], bk[1][:2 * L]), _TN))
        q["gt"] = _mm3(sp_uv0, bk, _TN)
        q["rhat"] = _split(q.pop("s_r") + _mm3((rbk[0][:, :2 * L], rbk[1][:, :2 * L]), sp_w))
        q["yhat"] = _mm3(rbk, sp_uv0)

    hts = [ht_ref[p] for p in range(n_pairs)]
    for q, (c, p) in zip(pre, items):
        sp_h = _split(hts[p])
        y_s = _mm3(q["rhat"], sp_h, _NT) + q["yhat"]
        hts[p] = (hts[p] + _mm3(sp_h, q["mt"]) + q["gt"]) * q["p_last"]
        y_ref[c * L:(c + 1) * L, p * LANES:(p + 1) * LANES] = y_s[:L] + y_s[L:]
    for p in range(n_pairs):
        ht_ref[p] = hts[p]

    ones_bd = _head_ones().astype(BF16)
    inv_n = 1.0 / HEAD_DIM
    for p in range(n_pairs):
        cs = slice(p * LANES, (p + 1) * LANES)
        y = y_ref[:, cs]
        mean = _dot_ones(y, ones_bd) * inv_n
        yc = y - mean
        var = _dot_ones(yc * yc, ones_bd) * inv_n
        yn = yc * lax.rsqrt(var + GN_EPS) * lnw_ref[:, cs] + lnb_ref[:, cs]
        r, k, v = r_ref[:, cs], k_ref[:, cs], v_ref[:, cs]
        bonus = _dot_ones(r * k * rk_ref[:, cs], ones_bd)
        o_ref[:, cs] = ((yn + bonus * v) * g_ref[:, cs]).astype(o_ref.dtype)


def rwkv_chunk(r, k, v, kk, b, cum, g, r_k, ln_w, ln_b, n_chunks=4, n_pairs=4):
    S, C = r.shape
    tm = n_chunks * CHUNK
    tw = n_pairs * LANES
    blk = pl.BlockSpec((tm, tw), lambda h, c: (c, h))
    vec = pl.BlockSpec((1, tw), lambda h, c: (0, h))
    return pl.pallas_call(
        functools.partial(_rwkv_chunk_kernel, n_chunks=n_chunks, n_pairs=n_pairs),
        out_shape=jax.ShapeDtypeStruct((S, C), BF16),
        grid=(C // tw, S // tm),
        in_specs=[blk] * 7 + [vec] * 3,
        out_specs=blk,
        scratch_shapes=[pltpu.VMEM((n_pairs, LANES, LANES), F32), pltpu.VMEM((tm, tw), F32)],
        compiler_params=_cparams(("parallel", "arbitrary")),
        name="rwkv_chunk",
    )(r, k, v, kk, b, cum, g, r_k, ln_w, ln_b)


def _gelu_tanh(y):
    return 0.5 * y * (1.0 + jnp.tanh(math.sqrt(2.0 / math.pi) * (y + 0.044715 * (y * y * y))))


def _split3(x):
    hi = x.astype(BF16)
    r1 = x - hi.astype(F32)
    mid = r1.astype(BF16)
    return hi, mid, (r1 - mid.astype(F32)).astype(BF16)


def _cmul(ar, ai, br, bi):
    return ar * br - ai * bi, ar * bi + ai * br


def _s5_kernel(u_ref, kall_ref, wz_ref, call_ref, pw_ref, ct_ref, d_ref, o_ref, carry_ref, ys_ref, *, tm, sub):
    half = 8 * S5_STATE
    Q = S5_LAG
    nq = tm // Q

    @pl.when(pl.program_id(1) == 0)
    def _():
        carry_ref[...] = jnp.zeros_like(carry_ref)

    zin = jnp.concatenate([u_ref[pl.ds(Q - 1 - tau, nq, stride=Q), :] for tau in range(Q)], axis=1)
    z = _mm3(_split(zin), (wz_ref[0, 0], wz_ref[0, 1]))

    xr, xi = z[:, :half], z[:, half:]
    crow = lax.broadcasted_iota(jnp.int32, (nq, 1), 0)
    for lvl in range(int(math.log2(nq))):
        off = 1 << lvl
        ar, ai = pw_ref[0, lvl:lvl + 1, :half], pw_ref[0, lvl:lvl + 1, half:]
        keep = crow >= off
        sr = jnp.where(keep, pltpu.roll(xr, off, axis=0), 0.0)
        si = jnp.where(keep, pltpu.roll(xi, off, axis=0), 0.0)
        xr, xi = xr + ar * sr - ai * si, xi + ar * si + ai * sr
    cr, ci = carry_ref[:, :half], carry_ref[:, half:]
    tr, ti = _cmul(ct_ref[0, :, :half], ct_ref[0, :, half:], cr, ci)
    xr, xi = xr + tr, xi + ti
    er = jnp.where(crow == 0, cr, pltpu.roll(xr, 1, axis=0))
    ei = jnp.where(crow == 0, ci, pltpu.roll(xi, 1, axis=0))
    carry_ref[:, :half] = xr[nq - 1:nq, :]
    carry_ref[:, half:] = xi[nq - 1:nq, :]

    ysf = _mm3(_split(jnp.concatenate([er, ei], axis=1)), (call_ref[0, 0], call_ref[0, 1]))
    for s in range(Q):
        ys_ref[pl.ds(s, nq, stride=Q), :] = ysf[:, s * LANES:(s + 1) * LANES]

    pos = lax.broadcasted_iota(jnp.int32, (sub, 1), 0) % Q
    for t0 in range(0, tm, sub):
        u = u_ref[t0:t0 + sub, :]
        lags = [u] + [jnp.where(pos >= tau, pltpu.roll(u, tau, axis=0), 0.0) for tau in range(1, Q)]
        y = _mm3(_split(jnp.concatenate(lags, axis=1)), (kall_ref[0, 0], kall_ref[0, 1]))
        o_ref[t0:t0 + sub, :] = _gelu_tanh(y + ys_ref[t0:t0 + sub, :] + d_ref[...] * u)


def s5_scan(p, col0, tables, d, tm, sub=256):
    kall, wz, call, pows, ctab = tables
    S = p.shape[0]
    nj = S5_WIDTH // LANES
    nst = 2 * 8 * S5_STATE
    nq = tm // S5_LAG
    ql = S5_LAG * LANES
    per_j = lambda *shape: pl.BlockSpec((1,) + shape, lambda j, i: (j,) + (0,) * len(shape))
    return pl.pallas_call(
        functools.partial(_s5_kernel, tm=tm, sub=sub),
        out_shape=jax.ShapeDtypeStruct((S, S5_WIDTH), F32),
        grid=(nj, S // tm),
        in_specs=[pl.BlockSpec((tm, LANES), lambda j, i: (i, col0 // LANES + j)),
                  per_j(2, ql, LANES), per_j(2, ql, nst), per_j(2, nst, ql),
                  per_j(pows.shape[1], nst), per_j(nq, nst),
                  pl.BlockSpec((1, LANES), lambda j, i: (0, j))],
        out_specs=pl.BlockSpec((tm, LANES), lambda j, i: (i, j)),
        scratch_shapes=[pltpu.VMEM((1, nst), F32), pltpu.VMEM((tm, LANES), F32)],
        compiler_params=_cparams(("parallel", "arbitrary")),
        name="s5_scan",
    )(p, kall, wz, call, pows, ctab, d)


def _s5_tables(lam_re, lam_im, log_step, b_re, b_im, c_re, c_im, tm):
    G, P, GS = S5_WIDTH // S5_GROUP, S5_STATE, S5_GROUP
    nj = G // 8
    Q = S5_LAG
    nq = tm // Q
    lam = lax.complex(jnp.minimum(lam_re, -1e-4), lam_im)
    step = jnp.exp(log_step)[:, None]
    lam_dt = lam * step
    lam_bar = jnp.exp(lam_dt)
    b_bar = ((lam_bar - 1.0) / lam)[..., None] * lax.complex(b_re, b_im)

    def lam_pow(e):
        return jnp.exp(lam_dt[None] * e.astype(F32)[:, None, None])

    def pack(z):
        z = jnp.moveaxis(z.reshape(z.shape[0], nj, 8 * P), 1, 0)
        return jnp.concatenate([jnp.real(z), jnp.imag(z)], axis=-1).astype(F32)

    def split(x):
        hi = x.astype(BF16)
        return jnp.stack([hi, (x - hi.astype(F32)).astype(BF16)], axis=1)

    pows = pack(lam_pow(Q * 2 ** jnp.arange(int(math.log2(nq)))))
    ctab = pack(lam_pow(Q * (jnp.arange(nq) + 1)))

    eye8 = jnp.eye(8, dtype=F32)
    m = lam_pow(jnp.arange(Q))[..., None] * b_bar[None]
    def bd_in(x):
        return jnp.einsum('tjgph,gk->jtghkp', x.reshape(Q, nj, 8, P, GS), eye8).reshape(nj, Q * 8 * GS, 8 * P)
    wz = jnp.concatenate([bd_in(jnp.real(m)), bd_in(jnp.imag(m))], axis=-1).astype(F32)
    ls = lam_pow(jnp.arange(Q) + 1)[:, :, None, :]
    lr, li = jnp.real(ls), jnp.imag(ls)
    def bd_out(x):
        return jnp.einsum('tjghp,gk->jkptgh', x.reshape(Q, nj, 8, GS, P), eye8).reshape(nj, 8 * P, Q * 8 * GS)
    call = jnp.concatenate([bd_out(c_re[None] * lr - c_im[None] * li),
                            -bd_out(c_re[None] * li + c_im[None] * lr)], axis=1).astype(F32)
    ktab = (jnp.einsum('ghp,tgpk->tghk', c_re, jnp.real(m), precision=HI)
            - jnp.einsum('ghp,tgpk->tghk', c_im, jnp.imag(m), precision=HI))
    kall = jnp.einsum('tjghk,gm->jtgkmh', ktab.reshape(Q, nj, 8, GS, GS), eye8).reshape(nj, Q * 8 * GS, 8 * GS)
    return split(kall.astype(F32)), split(wz), split(call), pows, ctab


def _glu_kernel(y_ref, yc_ref, w_ref, b_ref, o_ref, a_ref):
    @pl.when(pl.program_id(1) == 0)
    def _():
        a_ref[...] = y_ref[...].astype(BF16)

    z = _dot(a_ref[...], w_ref[...]) + b_ref[...]
    o_ref[...] = (yc_ref[...] * _sigmoid(z)).astype(o_ref.dtype)


def glu(y, w, b, tm=512, tn=512):
    S, C = y.shape
    return pl.pallas_call(
        _glu_kernel,
        out_shape=jax.ShapeDtypeStruct((S, C), BF16),
        grid=(S // tm, C // tn),
        in_specs=[pl.BlockSpec((tm, C), lambda i, j: (i, 0)), pl.BlockSpec((tm, tn), lambda i, j: (i, j)),
                  pl.BlockSpec((C, tn), lambda i, j: (0, j)), pl.BlockSpec((1, tn), lambda i, j: (0, j))],
        out_specs=pl.BlockSpec((tm, tn), lambda i, j: (i, j)),
        scratch_shapes=[pltpu.VMEM((tm, C), BF16)],
        compiler_params=_cparams(("parallel", "arbitrary")),
        name="s5_glu",
    )(y, y, w, b)


def _attn_block(qs, ks, vs, mask2, head0):
    zero = jnp.zeros((), qs.dtype)
    outs, lses = [], []
    for hmask in (head0, jnp.logical_not(head0)):
        s = jnp.where(mask2, _dot_nt(jnp.where(hmask, qs, zero), ks), -jnp.inf)
        m = jnp.max(s, axis=-1, keepdims=True)
        pexp = jnp.exp(s - m)
        den = jnp.sum(pexp, axis=-1, keepdims=True)
        outs.append(_dot(pexp.astype(vs.dtype), vs) / den)
        lses.append(m + jnp.log(den))
    return jnp.where(head0, outs[0], outs[1]), jnp.where(head0, lses[0], lses[1])


def _attn_kernel(*refs, dilations, tile):
    nb = len(dilations)
    in_refs = [refs[5 * b:5 * b + 5] for b in range(nb)]
    out_ref, o_s, l_s = refs[5 * nb:]
    B = ATTN_BLK
    first_tile = pl.program_id(1) == 0
    head0 = lax.broadcasted_iota(jnp.int32, (B, LANES), 1) < HEAD_DIM
    qi = lax.broadcasted_iota(jnp.int32, (B, 2 * B), 0)
    kj = lax.broadcasted_iota(jnp.int32, (B, 2 * B), 1)
    band = (kj >= qi) & (kj <= qi + B)
    band_first = band & (kj >= jnp.where(first_tile, B, 0))
    scale = HEAD_DIM ** -0.5

    for b, d in enumerate(dilations):
        q_ref, kc_ref, kp_ref, vc_ref, vp_ref = in_refs[b]
        n_blk = tile // d // B

        def block(r, row0, first_blk, b=b, d=d, refs_=in_refs[b]):
            q_ref, kc_ref, kp_ref, vc_ref, vp_ref = refs_
            qs = q_ref[r, pl.ds(row0, B), :] * scale
            if first_blk:
                ks = jnp.concatenate([kp_ref[r], kc_ref[r, :B, :]], axis=0)
                vs = jnp.concatenate([vp_ref[r], vc_ref[r, :B, :]], axis=0)
            else:
                ks = kc_ref[r, pl.ds(row0 - B, 2 * B), :]
                vs = vc_ref[r, pl.ds(row0 - B, 2 * B), :]
            o, lse = _attn_block(qs, ks, vs, band_first if first_blk else band, head0)
            dst = pl.ds(row0 * d + r, B, stride=d) if d > 1 else pl.ds(row0, B)
            o_s[b, dst, :] = o
            l_s[b, dst, :] = lse

        for r in range(d):
            for qb in range(n_blk):
                block(r, qb * B, qb == 0)

    rc = 2 * B

    def merge(c, carry):
        sl = pl.ds(pl.multiple_of(c * rc, rc), rc)
        ls_ = [l_s[b, sl, :] for b in range(nb)]
        m = functools.reduce(jnp.maximum, ls_)
        es = [jnp.exp(l - m) for l in ls_]
        num = functools.reduce(lambda x, y: x + y, [e * o_s[b, sl, :] for b, e in enumerate(es)])
        den = functools.reduce(lambda x, y: x + y, es)
        out_ref[sl, :] = (num / den).astype(out_ref.dtype)
        return carry
    lax.fori_loop(0, tile // rc, merge, 0)


def dilated_attention(qkvs, tile=ATTN_TILE):
    dilations = tuple(t.shape[0] for t in qkvs)
    S = qkvs[0].shape[0] * qkvs[0].shape[1]
    nh = D_MODEL // LANES
    in_specs, args = [], []
    for t, d in zip(qkvs, dilations):
        rows = tile // d
        rpb = rows // ATTN_BLK

        def cur(part, rows=rows, d=d):
            return pl.BlockSpec((d, rows, LANES), lambda h, i: (0, i, part * nh + h))

        def prev(part, rpb=rpb, d=d):
            return pl.BlockSpec((d, ATTN_BLK, LANES),
                                lambda h, i: (0, jnp.maximum(i * rpb - 1, 0), part * nh + h))

        in_specs += [cur(0), cur(1), prev(1), cur(2), prev(2)]
        args += [t] * 5
    nb = len(dilations)
    return pl.pallas_call(
        functools.partial(_attn_kernel, dilations=dilations, tile=tile),
        out_shape=jax.ShapeDtypeStruct((S, D_MODEL), BF16),
        grid=(nh, S // tile),
        in_specs=in_specs,
        out_specs=pl.BlockSpec((tile, LANES), lambda h, i: (i, h)),
        scratch_shapes=[pltpu.VMEM((nb, tile, LANES), F32), pltpu.VMEM((nb, tile, LANES), F32)],
        compiler_params=_cparams(("parallel", "arbitrary")),
        name="dilated_attention",
    )(*args)


def _rotary_tables(S):
    half = ROT_DIM // 2
    inv = ROPE_THETA ** (-jnp.arange(half, dtype=F32) * 2.0 / ROT_DIM)
    ang = jnp.arange(S, dtype=F32)[:, None] * inv[None, :]
    cos, sin = jnp.cos(ang), jnp.sin(ang)
    rest = HEAD_DIM - ROT_DIM
    ones = jnp.ones((S, rest), F32)
    zeros = jnp.zeros((S, rest), F32)
    zh = jnp.zeros((S, half), F32)
    cos_t = jnp.concatenate([cos, cos, ones], axis=1)
    sin_a = jnp.concatenate([-sin, zh, zeros], axis=1)
    sin_b = jnp.concatenate([zh, sin, zeros], axis=1)
    rep = LANES // HEAD_DIM
    return tuple(jnp.tile(t, (1, rep)) for t in (cos_t, sin_a, sin_b))


def _pad_rows(w, n):
    return jnp.pad(w, ((0, n - w.shape[0]), (0, 0)))


def _split_w(w):
    hi = w.astype(BF16)
    return jnp.stack([hi, (w - hi.astype(F32)).astype(BF16)])


def kernel(x, c, ada_w, ada_b, norm_mix_g, norm_ffn_g, hyb_w_in, hyb_w_out, rwkv_mu, rwkv_w0, rwkv_w2, rwkv_a0, rwkv_a2, rwkv_g2, rwkv_k_k, rwkv_k_a, rwkv_r_k, rwkv_ln_w, rwkv_ln_b, rwkv_v0, rwkv_v1, rwkv_v2, s5_lam_re, s5_lam_im, s5_log_step, s5_b_re, s5_b_im, s5_c_re, s5_c_im, s5_d, s5_glu_w, s5_glu_b, attn_w_qkv, attn_w_o, ffn_w_in, ffn_w_out, final_norm_g):
    B, S, D = x.shape
    assert B == 1 and D == D_MODEL and S % 2048 == 0
    C = RWKV_WIDTH
    xs = x.reshape(S, D)
    mod = ada_modulation(c, ada_w, ada_b)
    rot_tables = _rotary_tables(S)
    s5_tm = 2048
    v_first = None
    row = lambda t: t.reshape(1, -1)

    for i in range(DEPTH):
        m = mod[i]
        sh_mix, sc_mix, gt_mix, sh_ffn, sc_ffn, gt_ffn = [m[:, q * D:(q + 1) * D] for q in range(6)]
        if i % 2 == 0:
            j = i // 2
            w_in = hyb_w_in[j]
            gpad = RWKV_IN_PAD - RWKV_IN
            w_in = jnp.concatenate([w_in[:, :RWKV_IN], jnp.zeros((D, gpad), F32), w_in[:, RWKV_IN:]],
                                   axis=1).astype(BF16)
            mu = jnp.concatenate([rwkv_mu[j], jnp.zeros((gpad,), F32)]).reshape(1, RWKV_IN_PAD)
            p = nm_matmul(xs, row(norm_mix_g[i]), sh_mix, sc_mix, w_in, F32, tm=1024, tn=640)

            w2p = _split_w(jnp.concatenate([rwkv_w2[j], jnp.zeros((LORA_A, C), F32)], axis=0))
            a2p = _split_w(jnp.concatenate([jnp.zeros((LORA_W, C), F32), rwkv_a2[j]], axis=0))
            g2p = _split_w(_pad_rows(rwkv_g2[j], 2 * LANES))
            vres = None
            if j > 0:
                v1p = _split_w(jnp.pad(rwkv_v1[j - 1], ((0, 0), (0, LANES - LORA_V))))
                v2p = _split_w(_pad_rows(rwkv_v2[j - 1], LANES))
                vres = (v_first, row(rwkv_v0[j - 1]), v1p, v2p)
            r, k, v, kk, b, cum, g = rwkv_prep(p, mu, row(rwkv_w0[j]), row(rwkv_a0[j]), row(rwkv_k_k[j]),
                                               row(rwkv_k_a[j]), w2p, a2p, g2p, vres)
            if j == 0:
                v_first = v
            y_rwkv = rwkv_chunk(r, k, v, kk, b, cum, g, row(rwkv_r_k[j]), row(rwkv_ln_w[j]),
                                row(rwkv_ln_b[j]))

            s5_tabs = _s5_tables(s5_lam_re[j], s5_lam_im[j], s5_log_step[j], s5_b_re[j], s5_b_im[j],
                                 s5_c_re[j], s5_c_im[j], s5_tm)
            y_s5 = s5_scan(p, RWKV_IN_PAD, s5_tabs, row(s5_d[j]), tm=s5_tm)
            y_s5 = glu(y_s5, s5_glu_w[j].astype(BF16), row(s5_glu_b[j]))
            xs = proj_residual([y_rwkv, y_s5], hyb_w_out[j].astype(BF16), xs, gt_mix, tm=512, tn=D)
        else:
            j = i // 2
            dilations = tuple(d for (_, d) in DILATED_BRANCHES)
            qkvs = qkv_proj(xs, row(norm_mix_g[i]), sh_mix, sc_mix, attn_w_qkv[j].astype(BF16), rot_tables,
                            2 * D, dilations, tm=1024, tn=512)
            o = dilated_attention(qkvs)
            xs = proj_residual([o], attn_w_o[j].astype(BF16), xs, gt_mix, tm=512, tn=D)
        xs = ffn(xs, row(norm_ffn_g[i]), sh_ffn, sc_ffn, gt_ffn, ffn_w_in[i].astype(BF16),
                 ffn_w_out[i].astype(BF16), row(final_norm_g), final_norm=(i == DEPTH - 1))
    return xs.reshape(B, S, D)
```

```python
import functools
import math

import jax
import jax.numpy as jnp
from jax import lax
from jax.experimental import pallas as pl
from jax.experimental.pallas import tpu as pltpu

F32 = jnp.float32
BF16 = jnp.bfloat16
HI = lax.Precision.HIGHEST

D_MODEL = 2048
DEPTH = 4
HEAD_DIM = 64
RWKV_WIDTH = 1024
LORA_W = 64
LORA_A = 64
LORA_V = 32
LORA_G = 160
RWKV_IN = 3 * RWKV_WIDTH + LORA_W + LORA_A + LORA_G
RWKV_IN_PAD = 3456
S5_WIDTH = 1024
S5_GROUP = 16
S5_STATE = 64
S5_LAG = 8
EVEN_IN_PAD = RWKV_IN_PAD + S5_WIDTH
EVEN_IN_TILED = 4608
ATTN_HEADS = 32
ROT_DIM = 16
ROPE_THETA = 500000.0
DILATED_BRANCHES = ((128, 1), (512, 4), (2048, 16))
ATTN_BLK = 128
ATTN_TILE = 2048
FFN_HIDDEN = 5632
RMS_EPS = 1e-6
GN_EPS = 64e-5

LANES = 128
CHUNK = 64
VMEM_LIMIT = 48 * 1024 * 1024


def _cparams(sem):
    return pltpu.CompilerParams(dimension_semantics=sem, vmem_limit_bytes=VMEM_LIMIT)


def _sigmoid(x):
    return 1.0 / (1.0 + jnp.exp(-x))


def _dot(a, b, prec=None):
    return jnp.dot(a, b, preferred_element_type=F32, precision=prec)


def _dot_nt(a, b, prec=None):
    return lax.dot_general(a, b, (((1,), (1,)), ((), ())), preferred_element_type=F32, precision=prec)


def _dot_tn(a, b, prec=None):
    return lax.dot_general(a, b, (((0,), (0,)), ((), ())), preferred_element_type=F32, precision=prec)


def _norm_mod(x, g, shift, scale):
    ms = jnp.mean(x * x, axis=-1, keepdims=True)
    return (x * lax.rsqrt(ms + RMS_EPS) * g) * (1.0 + scale) + shift


def _ada_kernel(c_ref, w_ref, b_ref, o_ref):
    c = c_ref[...]
    s = c * _sigmoid(c)
    o_ref[0] = jnp.sum(s * w_ref[0], axis=0, keepdims=True) + b_ref[0]


def ada_modulation(c, ada_w, ada_b, tn=1024):
    L, D, N = ada_w.shape
    return pl.pallas_call(
        _ada_kernel,
        out_shape=jax.ShapeDtypeStruct((L, 1, N), F32),
        grid=(L, N // tn),
        in_specs=[pl.BlockSpec((D, 1), lambda l, j: (0, 0)),
                  pl.BlockSpec((1, D, tn), lambda l, j: (l, 0, j)),
                  pl.BlockSpec((1, 1, tn), lambda l, j: (l, 0, j))],
        out_specs=pl.BlockSpec((1, 1, tn), lambda l, j: (l, 0, j)),
        compiler_params=_cparams(("parallel", "parallel")),
        name="ada_modulation",
    )(c.reshape(D, 1), ada_w, ada_b.reshape(L, 1, N))


def _nm_matmul_kernel(x_ref, g_ref, sh_ref, sc_ref, w_ref, o_ref, h_ref):
    @pl.when(pl.program_id(1) == 0)
    def _():
        h_ref[...] = _norm_mod(x_ref[...], g_ref[...], sh_ref[...], sc_ref[...]).astype(BF16)

    o_ref[...] = _dot(h_ref[...], w_ref[...]).astype(o_ref.dtype)


def _nm_specs(x, w, tm, tn):
    D = x.shape[1]
    row = lambda i, j: (i, 0)
    vec = lambda i, j: (0, 0)
    return [pl.BlockSpec((tm, D), row), pl.BlockSpec((1, D), vec), pl.BlockSpec((1, D), vec),
            pl.BlockSpec((1, D), vec), pl.BlockSpec((D, tn), lambda i, j: (0, j))]


def nm_matmul(x, g, shift, scale, w, out_dtype, tm, tn):
    S, D = x.shape
    N = w.shape[1]
    return pl.pallas_call(
        _nm_matmul_kernel,
        out_shape=jax.ShapeDtypeStruct((S, N), out_dtype),
        grid=(S // tm, N // tn),
        in_specs=_nm_specs(x, w, tm, tn),
        out_specs=pl.BlockSpec((tm, tn), lambda i, j: (i, j)),
        scratch_shapes=[pltpu.VMEM((tm, D), BF16)],
        compiler_params=_cparams(("parallel", "arbitrary")),
        name="norm_mod_matmul",
    )(x, g, shift, scale, w)


def _qkv_kernel(x_ref, g_ref, sh_ref, sc_ref, w_ref, cos_ref, sa_ref, sb_ref, *rest, n_rot_blocks, dilations):
    o_refs, (h_ref, t_ref) = rest[:len(dilations)], rest[len(dilations):]
    j = pl.program_id(1)
    n_lane_blocks, tm, _ = t_ref.shape

    @pl.when(j == 0)
    def _():
        h_ref[...] = _norm_mod(x_ref[...], g_ref[...], sh_ref[...], sc_ref[...]).astype(BF16)

    acc = _dot(h_ref[...], w_ref[...])

    @pl.when(j < n_rot_blocks)
    def _():
        cos, sa, sb = cos_ref[...], sa_ref[...], sb_ref[...]
        half = ROT_DIM // 2
        for c in range(n_lane_blocks):
            t = acc[:, c * LANES:(c + 1) * LANES]
            t_ref[c] = t * cos + pltpu.roll(t, LANES - half, axis=1) * sa + pltpu.roll(t, half, axis=1) * sb

    @pl.when(j >= n_rot_blocks)
    def _():
        for c in range(n_lane_blocks):
            t_ref[c] = acc[:, c * LANES:(c + 1) * LANES]

    for d, o_ref in zip(dilations, o_refs):
        for c in range(n_lane_blocks):
            cs = slice(c * LANES, (c + 1) * LANES)
            if d == 1:
                o_ref[0, :, cs] = t_ref[c].astype(o_ref.dtype)
                continue
            for r in range(d):
                o_ref[r, :, cs] = t_ref[c, pl.ds(r, tm // d, stride=d), :].astype(o_ref.dtype)


def qkv_proj(x, g, shift, scale, w, rot_tables, n_rot_cols, dilations, tm, tn):
    S, D = x.shape
    N = w.shape[1]
    in_specs = _nm_specs(x, w, tm, tn) + [pl.BlockSpec((tm, LANES), lambda i, j: (i, 0))] * 3
    return pl.pallas_call(
        functools.partial(_qkv_kernel, n_rot_blocks=n_rot_cols // tn, dilations=dilations),
        out_shape=[jax.ShapeDtypeStruct((d, S // d, N), BF16) for d in dilations],
        grid=(S // tm, N // tn),
        in_specs=in_specs,
        out_specs=[pl.BlockSpec((d, tm // d, tn), lambda i, j: (0, i, j)) for d in dilations],
        scratch_shapes=[pltpu.VMEM((tm, D), BF16), pltpu.VMEM((tn // LANES, tm, LANES), F32)],
        compiler_params=_cparams(("parallel", "arbitrary")),
        name="qkv_proj",
    )(x, g, shift, scale, w, *rot_tables)


def _proj_res_kernel(*refs, n_a):
    a_refs, w_refs = refs[:n_a], refs[n_a:2 * n_a]
    x_ref, gt_ref, o_ref = refs[2 * n_a:]
    acc = _dot(a_refs[0][...], w_refs[0][...])
    for a_ref, w_ref in zip(a_refs[1:], w_refs[1:]):
        acc = acc + _dot(a_ref[...], w_ref[...])
    o_ref[...] = x_ref[...] + gt_ref[...] * acc


def proj_residual(a_list, w, x, gate, tm, tn):
    S, N = x.shape
    n_a = len(a_list)
    in_specs, args = [], []
    for a in a_list:
        in_specs.append(pl.BlockSpec((tm, a.shape[1]), lambda i, j: (i, 0)))
        args.append(a)
    off = 0
    for a in a_list:
        ka = a.shape[1]
        in_specs.append(pl.BlockSpec((ka, tn), functools.partial(lambda i, j, b: (b, j), b=off // ka)))
        args.append(w)
        off += ka
    in_specs += [pl.BlockSpec((tm, tn), lambda i, j: (i, j)), pl.BlockSpec((1, tn), lambda i, j: (0, j))]
    args += [x, gate]
    return pl.pallas_call(
        functools.partial(_proj_res_kernel, n_a=n_a),
        out_shape=jax.ShapeDtypeStruct((S, N), F32),
        grid=(S // tm, N // tn),
        in_specs=in_specs,
        out_specs=pl.BlockSpec((tm, tn), lambda i, j: (i, j)),
        compiler_params=_cparams(("parallel", "parallel")),
        name="proj_residual",
    )(*args)


def _ffn_kernel(x_ref, g_ref, sh_ref, sc_ref, gt_ref, wg_ref, wu_ref, wo_ref, fg_ref, o_ref, h_ref, acc_ref,
                *, final_norm):
    k = pl.program_id(1)

    @pl.when(k == 0)
    def _():
        h_ref[...] = _norm_mod(x_ref[...], g_ref[...], sh_ref[...], sc_ref[...]).astype(BF16)
        acc_ref[...] = jnp.zeros_like(acc_ref)

    h = h_ref[...]
    gate = _dot(h, wg_ref[...])
    up = _dot(h, wu_ref[...])
    act = (gate * _sigmoid(gate) * up).astype(BF16)
    acc_ref[...] += _dot(act, wo_ref[...])

    @pl.when(k == pl.num_programs(1) - 1)
    def _():
        xn = x_ref[...] + gt_ref[...] * acc_ref[...]
        if final_norm:
            ms = jnp.mean(xn * xn, axis=-1, keepdims=True)
            xn = xn * lax.rsqrt(ms + RMS_EPS) * fg_ref[...]
        o_ref[...] = xn


def ffn(x, g, shift, scale, gate, w_in, w_out, final_g, final_norm, tm=512, th=512):
    S, D = x.shape
    H = w_out.shape[0]
    nk = H // th
    row = lambda i, k: (i, 0)
    vec = lambda i, k: (0, 0)
    return pl.pallas_call(
        functools.partial(_ffn_kernel, final_norm=final_norm),
        out_shape=jax.ShapeDtypeStruct((S, D), F32),
        grid=(S // tm, nk),
        in_specs=[pl.BlockSpec((tm, D), row), pl.BlockSpec((1, D), vec), pl.BlockSpec((1, D), vec),
                  pl.BlockSpec((1, D), vec), pl.BlockSpec((1, D), vec),
                  pl.BlockSpec((D, th), lambda i, k: (0, k)),
                  pl.BlockSpec((D, th), lambda i, k: (0, k + nk)),
                  pl.BlockSpec((th, D), lambda i, k: (k, 0)),
                  pl.BlockSpec((1, D), vec)],
        out_specs=pl.BlockSpec((tm, D), row),
        scratch_shapes=[pltpu.VMEM((tm, D), BF16), pltpu.VMEM((tm, D), F32)],
        compiler_params=_cparams(("parallel", "arbitrary")),
        name="swiglu_ffn",
    )(x, g, shift, scale, gate, w_in, w_in, w_out, final_g)


def _segsum_heads(x, ones_bd):
    cols = [_dot_ones(x[:, c * LANES:(c + 1) * LANES], ones_bd) for c in range(x.shape[1] // LANES)]
    return jnp.concatenate(cols, axis=1)


def _head_ones():
    r = lax.broadcasted_iota(jnp.int32, (LANES, LANES), 0) // HEAD_DIM
    c = lax.broadcasted_iota(jnp.int32, (LANES, LANES), 1) // HEAD_DIM
    return jnp.where(r == c, 1.0, 0.0).astype(F32)


def _rwkv_prep_kernel(*refs, has_vres, tm):
    (p_ref, pp_ref, mu_ref, w0_ref, a0_ref, kk_w_ref, ka_ref, w2_ref, a2_ref, g2_ref) = refs[:10]
    if has_vres:
        vf_ref, v0_ref, v1_ref, v2_ref = refs[10:14]
        outs = refs[14:]
    else:
        outs = refs[10:]
    r_o, k_o, v_o, kk_o, b_o, cum_o, g_o = outs
    first = pl.program_id(0) == 0
    row = lax.broadcasted_iota(jnp.int32, (tm, 1), 0)

    def mixed(c0, c1):
        p = p_ref[:, c0:c1]
        prev_last = jnp.where(first, 0.0, pp_ref[7:8, c0:c1])
        sh = jnp.where(row == 0, prev_last, pltpu.roll(p, 1, axis=0))
        return p + (sh - p) * mu_ref[:, c0:c1]

    C = RWKV_WIDTH
    r = mixed(0, C)
    k = mixed(C, 2 * C)
    v = mixed(2 * C, 3 * C)
    wa = mixed(3 * C, 3 * C + LANES)
    gl = mixed(3 * C + LANES, RWKV_IN_PAD)

    lora = lambda act, w_ref: _mm3(_split(act), (w_ref[0], w_ref[1]))
    wlog = w0_ref[...] + lora(jnp.tanh(wa), w2_ref)
    xs = -wlog
    softplus = jnp.maximum(xs, 0.0) + jnp.log(1.0 + jnp.exp(-jnp.abs(xs)))
    w = -softplus - 0.5
    log_decay = -jnp.exp(w)
    a = _sigmoid(a0_ref[...] + lora(wa, a2_ref))
    g = lora(_sigmoid(gl), g2_ref)
    if has_vres:
        v = v + (vf_ref[...] - v) * _sigmoid(v0_ref[...] + lora(lora(v, v1_ref), v2_ref))

    kk = k * kk_w_ref[...]
    nrm = jnp.sqrt(_segsum_heads(kk * kk, _head_ones().astype(BF16)))
    kk = kk / jnp.maximum(nrm, 1e-12)
    k = k * (1.0 + (a - 1.0) * ka_ref[...])

    ti = lax.broadcasted_iota(jnp.int32, (tm, tm), 0)
    si = lax.broadcasted_iota(jnp.int32, (tm, tm), 1)
    tri = jnp.where((si <= ti) & (si // CHUNK == ti // CHUNK), 1.0, 0.0).astype(BF16)
    cum = sum(jnp.dot(tri, part, preferred_element_type=F32) for part in _split3(log_decay))

    r_o[...] = r
    k_o[...] = k
    v_o[...] = v
    kk_o[...] = kk
    b_o[...] = kk * a
    cum_o[...] = cum
    g_o[...] = g


def rwkv_prep(p, mu, w0, a0, k_k, k_a, w2p, a2p, g2p, vres, tm=256):
    S = p.shape[0]
    C = RWKV_WIDTH
    row = lambda i: (i, 0)
    vec = lambda i: (0, 0)
    whole = lambda w: pl.BlockSpec(w.shape, lambda i: (0,) * w.ndim)
    in_specs = [pl.BlockSpec((tm, RWKV_IN_PAD), row),
                pl.BlockSpec((8, RWKV_IN_PAD), lambda i: (jnp.maximum(i * (tm // 8) - 1, 0), 0)),
                pl.BlockSpec((1, RWKV_IN_PAD), vec),
                pl.BlockSpec((1, C), vec), pl.BlockSpec((1, C), vec), pl.BlockSpec((1, C), vec),
                pl.BlockSpec((1, C), vec), whole(w2p), whole(a2p), whole(g2p)]
    args = [p, p, mu, w0, a0, k_k, k_a, w2p, a2p, g2p]
    if vres is not None:
        v_first, v0, v1p, v2p = vres
        in_specs += [pl.BlockSpec((tm, C), row), pl.BlockSpec((1, C), vec), whole(v1p), whole(v2p)]
        args += [v_first, v0, v1p, v2p]
    out = jax.ShapeDtypeStruct((S, C), F32)
    return pl.pallas_call(
        functools.partial(_rwkv_prep_kernel, has_vres=vres is not None, tm=tm),
        out_shape=[out] * 7,
        grid=(S // tm,),
        in_specs=in_specs,
        out_specs=[pl.BlockSpec((tm, C), row)] * 7,
        compiler_params=_cparams(("parallel",)),
        name="rwkv_prep",
    )(*args)


_NN = (((1,), (0,)), ((), ()))
_NT = (((1,), (1,)), ((), ()))
_TN = (((0,), (0,)), ((), ()))


def _split(x):
    hi = x.astype(BF16)
    return hi, (x - hi.astype(F32)).astype(BF16)


def _mm3(a, b, dims=_NN):
    f = lambda x, y: lax.dot_general(x, y, dims, preferred_element_type=F32)
    return f(a[0], b[0]) + (f(a[0], b[1]) + f(a[1], b[0]))


def _dot_ones(x, ones_bf16):
    hi = x.astype(BF16)
    r1 = x - hi.astype(F32)
    mid = r1.astype(BF16)
    lo = (r1 - mid.astype(F32)).astype(BF16)
    f = lambda t: jnp.dot(t, ones_bf16, preferred_element_type=F32)
    return f(hi) + (f(mid) + f(lo))


def _rwkv_chunk_kernel(r_ref, k_ref, v_ref, kk_ref, b_ref, cum_ref, g_ref, rk_ref, lnw_ref, lnb_ref,
                       o_ref, ht_ref, y_ref, *, n_chunks, n_pairs):
    L = CHUNK

    @pl.when(pl.program_id(1) == 0)
    def _():
        ht_ref[...] = jnp.zeros_like(ht_ref)

    row = lax.broadcasted_iota(jnp.int32, (L, LANES), 0)
    lane = lax.broadcasted_iota(jnp.int32, (L, LANES), 1)
    head0 = lane < HEAD_DIM
    ri = lax.broadcasted_iota(jnp.int32, (2 * L, 2 * L), 0)
    ci = lax.broadcasted_iota(jnp.int32, (2 * L, 2 * L), 1)
    same = (ri // L) == (ci // L)
    strict = same & ((ci % L) < (ri % L))
    incl = same & ((ci % L) <= (ri % L))

    def stack(x):
        return jnp.concatenate([jnp.where(head0, x, 0.0), jnp.where(head0, 0.0, x)], axis=0)

    items = [(c, p) for c in range(n_chunks) for p in range(n_pairs)]
    pre = [dict() for _ in items]
    for q, (c, p) in zip(pre, items):
        sl = (slice(c * L, (c + 1) * L), slice(p * LANES, (p + 1) * LANES))
        r, k, v, kk, b, cum = r_ref[sl], k_ref[sl], v_ref[sl], kk_ref[sl], b_ref[sl], cum_ref[sl]
        cum_excl = jnp.where(row == 0, 0.0, pltpu.roll(cum, 1, axis=0))
        e_pos = jnp.exp(cum)
        e_neg = jnp.exp(-cum)
        q["s_a"] = stack(-kk * jnp.exp(cum_excl))
        q["v"] = stack(v)
        q["s_r"] = stack(r * e_pos)
        sp_ar = _split(jnp.concatenate([q["s_a"], q["s_r"]], axis=0))
        q["bk"] = _split(jnp.concatenate([stack(b * e_neg), stack(k * e_neg)], axis=0))
        q["p_last"] = e_pos[L - 1:L, :]
        q["big"] = _mm3(sp_ar, q["bk"], _NT)
    for q in pre:
        big = q.pop("big")
        q["pw"] = jnp.where(strict, big[:2 * L, :2 * L], 0.0)
        q["a_ak"] = jnp.where(strict, big[:2 * L, 2 * L:], 0.0)
        q["rbk"] = _split(jnp.concatenate([jnp.where(incl, big[2 * L:, :2 * L], 0.0),
                                           jnp.where(incl, big[2 * L:, 2 * L:], 0.0)], axis=1))
    mm = lambda x, y: _mm3(_split(x), _split(y))
    mmb = lambda x, y: jnp.dot(x.astype(BF16), y.astype(BF16), preferred_element_type=F32)
    m0 = 8
    blk_of = lambda m: ((ri // m) == (ci // m))
    for q in pre:
        d1 = jnp.where(blk_of(m0), q["pw"], 0.0)
        q["d1"] = d1
        q["d2"] = mmb(d1, d1)
    for q in pre:
        d1, d2 = q.pop("d1"), q["d2"]
        q["n"] = d1 + d2 + mmb(d1, d2)
        q["d4"] = mmb(d2, d2)
        q.pop("d2")
    for q in pre:
        d4 = q.pop("d4")
        q["n"] = q["n"] + d4 + mmb(q["n"], d4)
    m = m0
    while m < L:
        for q in pre:
            e = jnp.where(blk_of(2 * m) & jnp.logical_not(blk_of(m)), q["pw"], 0.0)
            q["x"] = e + mmb(q["n"], e)
        for q in pre:
            x = q.pop("x")
            q["n"] = q["n"] + x + mmb(x, q["n"])
        m *= 2
    for q in pre:
        q.pop("pw")
        q["tmp"] = mm(q.pop("a_ak"), q["v"])
    for q in pre:
        x = jnp.concatenate([q.pop("s_a"), q.pop("tmp")], axis=1)
        q["wu"] = x + mm(q.pop("n"), x)
    for q in pre:
        wu = q.pop("wu")
        sp_w = _split(wu[:, :2 * L])
        sp_uv0 = _split(jnp.concatenate([wu[:, 2 * L:], q.pop("v")], axis=0))
        bk, rbk = q.pop("bk"), q.pop("rbk")
        q["mt"] = _split(_mm3(sp_w, (bk[0][:2 * L], bk[1][:2 * L]), _TN))
        q["gt"] = _mm3(sp_uv0, bk, _TN)
        q["rhat"] = _split(q.pop("s_r") + _mm3((rbk[0][:, :2 * L], rbk[1][:, :2 * L]), sp_w))
        q["yhat"] = _mm3(rbk, sp_uv0)

    hts = [ht_ref[p] for p in range(n_pairs)]
    for q, (c, p) in zip(pre, items):
        sp_h = _split(hts[p])
        y_s = _mm3(q["rhat"], sp_h, _NT) + q["yhat"]
        hts[p] = (hts[p] + _mm3(sp_h, q["mt"]) + q["gt"]) * q["p_last"]
        y_ref[c * L:(c + 1) * L, p * LANES:(p + 1) * LANES] = y_s[:L] + y_s[L:]
    for p in range(n_pairs):
        ht_ref[p] = hts[p]

    ones_bd = _head_ones().astype(BF16)
    inv_n = 1.0 / HEAD_DIM
    for p in range(n_pairs):
        cs = slice(p * LANES, (p + 1) * LANES)
        y = y_ref[:, cs]
        mean = _dot_ones(y, ones_bd) * inv_n
        yc = y - mean
        var = _dot_ones(yc * yc, ones_bd) * inv_n
        yn = yc * lax.rsqrt(var + GN_EPS) * lnw_ref[:, cs] + lnb_ref[:, cs]
        r, k, v = r_ref[:, cs], k_ref[:, cs], v_ref[:, cs]
        bonus = _dot_ones(r * k * rk_ref[:, cs], ones_bd)
        o_ref[:, cs] = ((yn + bonus * v) * g_ref[:, cs]).astype(o_ref.dtype)


def rwkv_chunk(r, k, v, kk, b, cum, g, r_k, ln_w, ln_b, n_chunks=4, n_pairs=4):
    S, C = r.shape
    tm = n_chunks * CHUNK
    tw = n_pairs * LANES
    blk = pl.BlockSpec((tm, tw), lambda h, c: (c, h))
    vec = pl.BlockSpec((1, tw), lambda h, c: (0, h))
    return pl.pallas_call(
        functools.partial(_rwkv_chunk_kernel, n_chunks=n_chunks, n_pairs=n_pairs),
        out_shape=jax.ShapeDtypeStruct((S, C), BF16),
        grid=(C // tw, S // tm),
        in_specs=[blk] * 7 + [vec] * 3,
        out_specs=blk,
        scratch_shapes=[pltpu.VMEM((n_pairs, LANES, LANES), F32), pltpu.VMEM((tm, tw), F32)],
        compiler_params=_cparams(("parallel", "arbitrary")),
        name="rwkv_chunk",
    )(r, k, v, kk, b, cum, g, r_k, ln_w, ln_b)


def _gelu_tanh(y):
    return 0.5 * y * (1.0 + jnp.tanh(math.sqrt(2.0 / math.pi) * (y + 0.044715 * (y * y * y))))


def _split3(x):
    hi = x.astype(BF16)
    r1 = x - hi.astype(F32)
    mid = r1.astype(BF16)
    return hi, mid, (r1 - mid.astype(F32)).astype(BF16)


def _cmul(ar, ai, br, bi):
    return ar * br - ai * bi, ar * bi + ai * br


def _s5_kernel(u_ref, kall_ref, wz_ref, call_ref, pw_ref, ct_ref, d_ref, o_ref, carry_ref, ys_ref, *, tm, sub):
    half = 8 * S5_STATE
    Q = S5_LAG
    nq = tm // Q

    @pl.when(pl.program_id(1) == 0)
    def _():
        carry_ref[...] = jnp.zeros_like(carry_ref)

    zin = jnp.concatenate([u_ref[pl.ds(Q - 1 - tau, nq, stride=Q), :] for tau in range(Q)], axis=1)
    z = jnp.dot(zin.astype(BF16), wz_ref[0], preferred_element_type=F32)

    xr, xi = z[:, :half], z[:, half:]
    crow = lax.broadcasted_iota(jnp.int32, (nq, 1), 0)
    for lvl in range(int(math.log2(nq))):
        off = 1 << lvl
        ar, ai = pw_ref[0, lvl:lvl + 1, :half], pw_ref[0, lvl:lvl + 1, half:]
        keep = crow >= off
        sr = jnp.where(keep, pltpu.roll(xr, off, axis=0), 0.0)
        si = jnp.where(keep, pltpu.roll(xi, off, axis=0), 0.0)
        xr, xi = xr + ar * sr - ai * si, xi + ar * si + ai * sr
    cr, ci = carry_ref[:, :half], carry_ref[:, half:]
    tr, ti = _cmul(ct_ref[0, :, :half], ct_ref[0, :, half:], cr, ci)
    xr, xi = xr + tr, xi + ti
    er = jnp.where(crow == 0, cr, pltpu.roll(xr, 1, axis=0))
    ei = jnp.where(crow == 0, ci, pltpu.roll(xi, 1, axis=0))
    carry_ref[:, :half] = xr[nq - 1:nq, :]
    carry_ref[:, half:] = xi[nq - 1:nq, :]

    ysf = jnp.dot(jnp.concatenate([er, ei], axis=1).astype(BF16), call_ref[0],
                  preferred_element_type=F32)
    for s in range(Q):
        ys_ref[pl.ds(s, nq, stride=Q), :] = ysf[:, s * LANES:(s + 1) * LANES]

    pos = lax.broadcasted_iota(jnp.int32, (sub, 1), 0) % Q
    for t0 in range(0, tm, sub):
        u = u_ref[t0:t0 + sub, :]
        lags = [u] + [jnp.where(pos >= tau, pltpu.roll(u, tau, axis=0), 0.0) for tau in range(1, Q)]
        y = jnp.dot(jnp.concatenate(lags, axis=1).astype(BF16), kall_ref[0], preferred_element_type=F32)
        o_ref[t0:t0 + sub, :] = _gelu_tanh(y + ys_ref[t0:t0 + sub, :] + d_ref[...] * u)


def s5_scan(p, col0, tables, d, tm, sub=256):
    kall, wz, call, pows, ctab = tables
    S = p.shape[0]
    nj = S5_WIDTH // LANES
    nst = 2 * 8 * S5_STATE
    nq = tm // S5_LAG
    ql = S5_LAG * LANES
    per_j = lambda *shape: pl.BlockSpec((1,) + shape, lambda j, i: (j,) + (0,) * len(shape))
    return pl.pallas_call(
        functools.partial(_s5_kernel, tm=tm, sub=sub),
        out_shape=jax.ShapeDtypeStruct((S, S5_WIDTH), F32),
        grid=(nj, S // tm),
        in_specs=[pl.BlockSpec((tm, LANES), lambda j, i: (i, col0 // LANES + j)),
                  per_j(ql, LANES), per_j(ql, nst), per_j(nst, ql),
                  per_j(pows.shape[1], nst), per_j(nq, nst),
                  pl.BlockSpec((1, LANES), lambda j, i: (0, j))],
        out_specs=pl.BlockSpec((tm, LANES), lambda j, i: (i, j)),
        scratch_shapes=[pltpu.VMEM((1, nst), F32), pltpu.VMEM((tm, LANES), F32)],
        compiler_params=_cparams(("parallel", "arbitrary")),
        name="s5_scan",
    )(p, kall, wz, call, pows, ctab, d)


def _s5_tables(lam_re, lam_im, log_step, b_re, b_im, c_re, c_im, tm):
    G, P, GS = S5_WIDTH // S5_GROUP, S5_STATE, S5_GROUP
    nj = G // 8
    Q = S5_LAG
    nq = tm // Q
    lam = lax.complex(jnp.minimum(lam_re, -1e-4), lam_im)
    step = jnp.exp(log_step)[:, None]
    lam_dt = lam * step
    lam_bar = jnp.exp(lam_dt)
    b_bar = ((lam_bar - 1.0) / lam)[..., None] * lax.complex(b_re, b_im)

    def lam_pow(e):
        return jnp.exp(lam_dt[None] * e.astype(F32)[:, None, None])

    def pack(z):
        z = jnp.moveaxis(z.reshape(z.shape[0], nj, 8 * P), 1, 0)
        return jnp.concatenate([jnp.real(z), jnp.imag(z)], axis=-1).astype(F32)

    pows = pack(lam_pow(Q * 2 ** jnp.arange(int(math.log2(nq)))))
    ctab = pack(lam_pow(Q * (jnp.arange(nq) + 1)))

    eye8 = jnp.eye(8, dtype=F32)
    m = lam_pow(jnp.arange(Q))[..., None] * b_bar[None]
    def bd_in(x):
        return jnp.einsum('tjgph,gk->jtghkp', x.reshape(Q, nj, 8, P, GS), eye8).reshape(nj, Q * 8 * GS, 8 * P)
    wz = jnp.concatenate([bd_in(jnp.real(m)), bd_in(jnp.imag(m))], axis=-1).astype(F32)
    ls = lam_pow(jnp.arange(Q) + 1)[:, :, None, :]
    lr, li = jnp.real(ls), jnp.imag(ls)
    def bd_out(x):
        return jnp.einsum('tjghp,gk->jkptgh', x.reshape(Q, nj, 8, GS, P), eye8).reshape(nj, 8 * P, Q * 8 * GS)
    call = jnp.concatenate([bd_out(c_re[None] * lr - c_im[None] * li),
                            -bd_out(c_re[None] * li + c_im[None] * lr)], axis=1).astype(F32)
    ktab = (jnp.einsum('ghp,tgpk->tghk', c_re, jnp.real(m), precision=HI)
            - jnp.einsum('ghp,tgpk->tghk', c_im, jnp.imag(m), precision=HI))
    kall = jnp.einsum('tjghk,gm->jtgkmh', ktab.reshape(Q, nj, 8, GS, GS), eye8).reshape(nj, Q * 8 * GS, 8 * GS)
    return kall.astype(BF16), wz.astype(BF16), call.astype(BF16), pows, ctab


def _glu_kernel(y_ref, yc_ref, w_ref, b_ref, o_ref, a_ref):
    @pl.when(pl.program_id(1) == 0)
    def _():
        a_ref[...] = y_ref[...].astype(BF16)

    z = _dot(a_ref[...], w_ref[...]) + b_ref[...]
    o_ref[...] = (yc_ref[...] * _sigmoid(z)).astype(o_ref.dtype)


def glu(y, w, b, tm=512, tn=512):
    S, C = y.shape
    return pl.pallas_call(
        _glu_kernel,
        out_shape=jax.ShapeDtypeStruct((S, C), BF16),
        grid=(S // tm, C // tn),
        in_specs=[pl.BlockSpec((tm, C), lambda i, j: (i, 0)), pl.BlockSpec((tm, tn), lambda i, j: (i, j)),
                  pl.BlockSpec((C, tn), lambda i, j: (0, j)), pl.BlockSpec((1, tn), lambda i, j: (0, j))],
        out_specs=pl.BlockSpec((tm, tn), lambda i, j: (i, j)),
        scratch_shapes=[pltpu.VMEM((tm, C), BF16)],
        compiler_params=_cparams(("parallel", "arbitrary")),
        name="s5_glu",
    )(y, y, w, b)


def _attn_block(qs, ks, vs, mask2, head0):
    zero = jnp.zeros((), qs.dtype)
    outs, lses = [], []
    for hmask in (head0, jnp.logical_not(head0)):
        s = jnp.where(mask2, _dot_nt(jnp.where(hmask, qs, zero), ks), -jnp.inf)
        m = jnp.max(s, axis=-1, keepdims=True)
        pexp = jnp.exp(s - m)
        den = jnp.sum(pexp, axis=-1, keepdims=True)
        outs.append(_dot(pexp.astype(vs.dtype), vs) / den)
        lses.append(m + jnp.log(den))
    return jnp.where(head0, outs[0], outs[1]), jnp.where(head0, lses[0], lses[1])


def _attn_kernel(*refs, dilations, tile):
    nb = len(dilations)
    in_refs = [refs[5 * b:5 * b + 5] for b in range(nb)]
    out_ref, o_s, l_s = refs[5 * nb:]
    B = ATTN_BLK
    first_tile = pl.program_id(1) == 0
    head0 = lax.broadcasted_iota(jnp.int32, (B, LANES), 1) < HEAD_DIM
    qi = lax.broadcasted_iota(jnp.int32, (B, 2 * B), 0)
    kj = lax.broadcasted_iota(jnp.int32, (B, 2 * B), 1)
    band = (kj >= qi) & (kj <= qi + B)
    band_first = band & (kj >= jnp.where(first_tile, B, 0))
    scale = HEAD_DIM ** -0.5

    for b, d in enumerate(dilations):
        q_ref, kc_ref, kp_ref, vc_ref, vp_ref = in_refs[b]
        n_blk = tile // d // B

        def block(r, row0, first_blk, b=b, d=d, refs_=in_refs[b]):
            q_ref, kc_ref, kp_ref, vc_ref, vp_ref = refs_
            qs = q_ref[r, pl.ds(row0, B), :] * scale
            if first_blk:
                ks = jnp.concatenate([kp_ref[r], kc_ref[r, :B, :]], axis=0)
                vs = jnp.concatenate([vp_ref[r], vc_ref[r, :B, :]], axis=0)
            else:
                ks = kc_ref[r, pl.ds(row0 - B, 2 * B), :]
                vs = vc_ref[r, pl.ds(row0 - B, 2 * B), :]
            o, lse = _attn_block(qs, ks, vs, band_first if first_blk else band, head0)
            dst = pl.ds(row0 * d + r, B, stride=d) if d > 1 else pl.ds(row0, B)
            o_s[b, dst, :] = o
            l_s[b, dst, :] = lse

        for r in range(d):
            for qb in range(n_blk):
                block(r, qb * B, qb == 0)

    rc = 2 * B

    def merge(c, carry):
        sl = pl.ds(pl.multiple_of(c * rc, rc), rc)
        ls_ = [l_s[b, sl, :] for b in range(nb)]
        m = functools.reduce(jnp.maximum, ls_)
        es = [jnp.exp(l - m) for l in ls_]
        num = functools.reduce(lambda x, y: x + y, [e * o_s[b, sl, :] for b, e in enumerate(es)])
        den = functools.reduce(lambda x, y: x + y, es)
        out_ref[sl, :] = (num / den).astype(out_ref.dtype)
        return carry
    lax.fori_loop(0, tile // rc, merge, 0)


def dilated_attention(qkvs, tile=ATTN_TILE):
    dilations = tuple(t.shape[0] for t in qkvs)
    S = qkvs[0].shape[0] * qkvs[0].shape[1]
    nh = D_MODEL // LANES
    in_specs, args = [], []
    for t, d in zip(qkvs, dilations):
        rows = tile // d
        rpb = rows // ATTN_BLK

        def cur(part, rows=rows, d=d):
            return pl.BlockSpec((d, rows, LANES), lambda h, i: (0, i, part * nh + h))

        def prev(part, rpb=rpb, d=d):
            return pl.BlockSpec((d, ATTN_BLK, LANES),
                                lambda h, i: (0, jnp.maximum(i * rpb - 1, 0), part * nh + h))

        in_specs += [cur(0), cur(1), prev(1), cur(2), prev(2)]
        args += [t] * 5
    nb = len(dilations)
    return pl.pallas_call(
        functools.partial(_attn_kernel, dilations=dilations, tile=tile),
        out_shape=jax.ShapeDtypeStruct((S, D_MODEL), BF16),
        grid=(nh, S // tile),
        in_specs=in_specs,
        out_specs=pl.BlockSpec((tile, LANES), lambda h, i: (i, h)),
        scratch_shapes=[pltpu.VMEM((nb, tile, LANES), F32), pltpu.VMEM((nb, tile, LANES), F32)],
        compiler_params=_cparams(("parallel", "arbitrary")),
        name="dilated_attention",
    )(*args)


def _rotary_tables(S):
    half = ROT_DIM // 2
    inv = ROPE_THETA ** (-jnp.arange(half, dtype=F32) * 2.0 / ROT_DIM)
    ang = jnp.arange(S, dtype=F32)[:, None] * inv[None, :]
    cos, sin = jnp.cos(ang), jnp.sin(ang)
    rest = HEAD_DIM - ROT_DIM
    ones = jnp.ones((S, rest), F32)
    zeros = jnp.zeros((S, rest), F32)
    zh = jnp.zeros((S, half), F32)
    cos_t = jnp.concatenate([cos, cos, ones], axis=1)
    sin_a = jnp.concatenate([-sin, zh, zeros], axis=1)
    sin_b = jnp.concatenate([zh, sin, zeros], axis=1)
    rep = LANES // HEAD_DIM
    return tuple(jnp.tile(t, (1, rep)) for t in (cos_t, sin_a, sin_b))


def _pad_rows(w, n):
    return jnp.pad(w, ((0, n - w.shape[0]), (0, 0)))


def _split_w(w):
    hi = w.astype(BF16)
    return jnp.stack([hi, (w - hi.astype(F32)).astype(BF16)])


def kernel(x, c, ada_w, ada_b, norm_mix_g, norm_ffn_g, hyb_w_in, hyb_w_out, rwkv_mu, rwkv_w0, rwkv_w2, rwkv_a0, rwkv_a2, rwkv_g2, rwkv_k_k, rwkv_k_a, rwkv_r_k, rwkv_ln_w, rwkv_ln_b, rwkv_v0, rwkv_v1, rwkv_v2, s5_lam_re, s5_lam_im, s5_log_step, s5_b_re, s5_b_im, s5_c_re, s5_c_im, s5_d, s5_glu_w, s5_glu_b, attn_w_qkv, attn_w_o, ffn_w_in, ffn_w_out, final_norm_g):
    B, S, D = x.shape
    assert B == 1 and D == D_MODEL and S % 2048 == 0
    C = RWKV_WIDTH
    xs = x.reshape(S, D)
    mod = ada_modulation(c, ada_w, ada_b)
    rot_tables = _rotary_tables(S)
    s5_tm = 2048
    v_first = None
    row = lambda t: t.reshape(1, -1)

    for i in range(DEPTH):
        m = mod[i]
        sh_mix, sc_mix, gt_mix, sh_ffn, sc_ffn, gt_ffn = [m[:, q * D:(q + 1) * D] for q in range(6)]
        if i % 2 == 0:
            j = i // 2
            w_in = hyb_w_in[j]
            gpad = RWKV_IN_PAD - RWKV_IN
            w_in = jnp.concatenate([w_in[:, :RWKV_IN], jnp.zeros((D, gpad), F32), w_in[:, RWKV_IN:],
                                    jnp.zeros((D, EVEN_IN_TILED - EVEN_IN_PAD), F32)], axis=1).astype(BF16)
            mu = jnp.concatenate([rwkv_mu[j], jnp.zeros((gpad,), F32)]).reshape(1, RWKV_IN_PAD)
            p = nm_matmul(xs, row(norm_mix_g[i]), sh_mix, sc_mix, w_in, F32, tm=1024, tn=768)

            w2p = _split_w(jnp.concatenate([rwkv_w2[j], jnp.zeros((LORA_A, C), F32)], axis=0))
            a2p = _split_w(jnp.concatenate([jnp.zeros((LORA_W, C), F32), rwkv_a2[j]], axis=0))
            g2p = _split_w(_pad_rows(rwkv_g2[j], 2 * LANES))
            vres = None
            if j > 0:
                v1p = _split_w(jnp.pad(rwkv_v1[j - 1], ((0, 0), (0, LANES - LORA_V))))
                v2p = _split_w(_pad_rows(rwkv_v2[j - 1], LANES))
                vres = (v_first, row(rwkv_v0[j - 1]), v1p, v2p)
            r, k, v, kk, b, cum, g = rwkv_prep(p, mu, row(rwkv_w0[j]), row(rwkv_a0[j]), row(rwkv_k_k[j]),
                                               row(rwkv_k_a[j]), w2p, a2p, g2p, vres)
            if j == 0:
                v_first = v
            y_rwkv = rwkv_chunk(r, k, v, kk, b, cum, g, row(rwkv_r_k[j]), row(rwkv_ln_w[j]),
                                row(rwkv_ln_b[j]))

            s5_tabs = _s5_tables(s5_lam_re[j], s5_lam_im[j], s5_log_step[j], s5_b_re[j], s5_b_im[j],
                                 s5_c_re[j], s5_c_im[j], s5_tm)
            y_s5 = s5_scan(p, RWKV_IN_PAD, s5_tabs, row(s5_d[j]), tm=s5_tm)
            y_s5 = glu(y_s5, s5_glu_w[j].astype(BF16), row(s5_glu_b[j]))
            xs = proj_residual([y_rwkv, y_s5], hyb_w_out[j].astype(BF16), xs, gt_mix, tm=512, tn=D)
        else:
            j = i // 2
            dilations = tuple(d for (_, d) in DILATED_BRANCHES)
            qkvs = qkv_proj(xs, row(norm_mix_g[i]), sh_mix, sc_mix, attn_w_qkv[j].astype(BF16), rot_tables,
                            2 * D, dilations, tm=1024, tn=512)
            o = dilated_attention(qkvs)
            xs = proj_residual([o], attn_w_o[j].astype(BF16), xs, gt_mix, tm=512, tn=D)
        xs = ffn(xs, row(norm_ffn_g[i]), sh_ffn, sc_ffn, gt_ffn, ffn_w_in[i].astype(BF16),
                 ffn_w_out[i].astype(BF16), row(final_norm_g), final_norm=(i == DEPTH - 1))
    return xs.reshape(B, S, D)
```

```python
import functools
import math

import jax
import jax.numpy as jnp
from jax import lax
from jax.experimental import pallas as pl
from jax.experimental.pallas import tpu as pltpu

F32 = jnp.float32
BF16 = jnp.bfloat16
HI = lax.Precision.HIGHEST

D_MODEL = 2048
DEPTH = 4
HEAD_DIM = 64
RWKV_WIDTH = 1024
LORA_W = 64
LORA_A = 64
LORA_V = 32
LORA_G = 160
RWKV_IN = 3 * RWKV_WIDTH + LORA_W + LORA_A + LORA_G
RWKV_IN_PAD = 3456
S5_WIDTH = 1024
S5_GROUP = 16
S5_STATE = 64
S5_LAG = 8
EVEN_IN_PAD = RWKV_IN_PAD + S5_WIDTH
EVEN_IN_TILED = 4608
ATTN_HEADS = 32
ROT_DIM = 16
ROPE_THETA = 500000.0
DILATED_BRANCHES = ((128, 1), (512, 4), (2048, 16))
ATTN_BLK = 128
ATTN_TILE = 2048
FFN_HIDDEN = 5632
RMS_EPS = 1e-6
GN_EPS = 64e-5

LANES = 128
CHUNK = 64
VMEM_LIMIT = 48 * 1024 * 1024

TILES = {
    "ada": 1024,
    "norm_rows": 512,
    "in_proj": (1024, 768),
    "qkv_proj": (1024, 512),
    "out_proj": 512,
    "ffn": (512, 512),
    "glu": (512, 512),
    "rwkv_prep": 256,
    "rwkv_chunk": (4, 4),
    "s5": (2048, 256),
}


def _cparams(sem):
    return pltpu.CompilerParams(dimension_semantics=sem, vmem_limit_bytes=VMEM_LIMIT)


def _sigmoid(x):
    return 1.0 / (1.0 + jnp.exp(-x))


def _dot(a, b, prec=None):
    return jnp.dot(a, b, preferred_element_type=F32, precision=prec)


def _dot_nt(a, b, prec=None):
    return lax.dot_general(a, b, (((1,), (1,)), ((), ())), preferred_element_type=F32, precision=prec)


def _dot_tn(a, b, prec=None):
    return lax.dot_general(a, b, (((0,), (0,)), ((), ())), preferred_element_type=F32, precision=prec)


def _norm_mod(x, g, shift, scale):
    ms = jnp.mean(x * x, axis=-1, keepdims=True)
    return (x * lax.rsqrt(ms + RMS_EPS) * g) * (1.0 + scale) + shift


def _ada_kernel(c_ref, w_ref, b_ref, o_ref):
    c = c_ref[...]
    s = c * _sigmoid(c)
    o_ref[0] = jnp.sum(s * w_ref[0], axis=0, keepdims=True) + b_ref[0]


def ada_modulation(c, ada_w, ada_b):
    L, D, N = ada_w.shape
    tn = TILES["ada"]
    return pl.pallas_call(
        _ada_kernel,
        out_shape=jax.ShapeDtypeStruct((L, 1, N), F32),
        grid=(L, N // tn),
        in_specs=[pl.BlockSpec((D, 1), lambda l, j: (0, 0)),
                  pl.BlockSpec((1, D, tn), lambda l, j: (l, 0, j)),
                  pl.BlockSpec((1, 1, tn), lambda l, j: (l, 0, j))],
        out_specs=pl.BlockSpec((1, 1, tn), lambda l, j: (l, 0, j)),
        compiler_params=_cparams(("parallel", "parallel")),
        name="ada_modulation",
    )(c.reshape(D, 1), ada_w, ada_b.reshape(L, 1, N))


def _norm_mod_kernel(x_ref, g_ref, sh_ref, sc_ref, h_ref):
    h_ref[...] = _norm_mod(x_ref[...], g_ref[...], sh_ref[...], sc_ref[...]).astype(BF16)


def norm_mod_rows(x, norm):
    S, D = x.shape
    tm = TILES["norm_rows"]
    vec = pl.BlockSpec((1, D), lambda i: (0, 0))
    return pl.pallas_call(
        _norm_mod_kernel,
        out_shape=jax.ShapeDtypeStruct((S, D), BF16),
        grid=(S // tm,),
        in_specs=[pl.BlockSpec((tm, D), lambda i: (i, 0)), vec, vec, vec],
        out_specs=pl.BlockSpec((tm, D), lambda i: (i, 0)),
        compiler_params=_cparams(("parallel",)),
        name="norm_mod_rows",
    )(x, *norm)


def _matmul_kernel(h_ref, w_ref, o_ref):
    o_ref[...] = _dot(h_ref[...], w_ref[...]).astype(o_ref.dtype)


def in_proj(h, w, layer):
    S, D = h.shape
    N = w.shape[2]
    tm, tn = TILES["in_proj"]
    return pl.pallas_call(
        _matmul_kernel,
        out_shape=jax.ShapeDtypeStruct((S, N), F32),
        grid=(S // tm, N // tn),
        in_specs=[pl.BlockSpec((tm, D), lambda i, j: (i, 0)),
                  pl.BlockSpec((None, D, tn), lambda i, j: (layer, 0, j))],
        out_specs=pl.BlockSpec((tm, tn), lambda i, j: (i, j)),
        compiler_params=_cparams(("parallel", "parallel")),
        name="in_proj",
    )(h, w)


def _qkv_kernel(h_ref, w_ref, cos_ref, sa_ref, sb_ref, *rest, n_rot_blocks, dilations):
    o_refs, t_ref = rest[:len(dilations)], rest[len(dilations)]
    j = pl.program_id(1)
    n_lane_blocks, tm, _ = t_ref.shape
    acc = _dot(h_ref[...], w_ref[...])

    @pl.when(j < n_rot_blocks)
    def _():
        cos, sa, sb = cos_ref[...], sa_ref[...], sb_ref[...]
        half = ROT_DIM // 2
        for c in range(n_lane_blocks):
            t = acc[:, c * LANES:(c + 1) * LANES]
            t_ref[c] = t * cos + pltpu.roll(t, LANES - half, axis=1) * sa + pltpu.roll(t, half, axis=1) * sb

    @pl.when(j >= n_rot_blocks)
    def _():
        for c in range(n_lane_blocks):
            t_ref[c] = acc[:, c * LANES:(c + 1) * LANES]

    for d, o_ref in zip(dilations, o_refs):
        for c in range(n_lane_blocks):
            cs = slice(c * LANES, (c + 1) * LANES)
            if d == 1:
                o_ref[0, :, cs] = t_ref[c].astype(o_ref.dtype)
                continue
            for r in range(d):
                o_ref[r, :, cs] = t_ref[c, pl.ds(r, tm // d, stride=d), :].astype(o_ref.dtype)


def qkv_proj(h, w, layer, rot_tables, n_rot_cols, dilations):
    S, D = h.shape
    N = w.shape[2]
    tm, tn = TILES["qkv_proj"]
    return pl.pallas_call(
        functools.partial(_qkv_kernel, n_rot_blocks=n_rot_cols // tn, dilations=dilations),
        out_shape=[jax.ShapeDtypeStruct((d, S // d, N), BF16) for d in dilations],
        grid=(S // tm, N // tn),
        in_specs=[pl.BlockSpec((tm, D), lambda i, j: (i, 0)),
                  pl.BlockSpec((None, D, tn), lambda i, j: (layer, 0, j))]
                 + [pl.BlockSpec((tm, LANES), lambda i, j: (i, 0))] * 3,
        out_specs=[pl.BlockSpec((d, tm // d, tn), lambda i, j: (0, i, j)) for d in dilations],
        scratch_shapes=[pltpu.VMEM((tn // LANES, tm, LANES), F32)],
        compiler_params=_cparams(("parallel", "parallel")),
        name="qkv_proj",
    )(h, w, *rot_tables)


def _proj_res_kernel(*refs, n_a):
    a_refs, w_refs = refs[:n_a], refs[n_a:2 * n_a]
    x_ref, gt_ref, g_ref, sh_ref, sc_ref, o_ref, h_ref = refs[2 * n_a:]
    acc = _dot(a_refs[0][...], w_refs[0][...])
    for a_ref, w_ref in zip(a_refs[1:], w_refs[1:]):
        acc = acc + _dot(a_ref[...], w_ref[...])
    xn = x_ref[...] + gt_ref[...] * acc
    o_ref[...] = xn
    h_ref[...] = _norm_mod(xn, g_ref[...], sh_ref[...], sc_ref[...]).astype(BF16)


def proj_residual(a_list, w, layer, x, gate, next_norm):
    S, N = x.shape
    tm = TILES["out_proj"]
    n_a = len(a_list)
    row = lambda i: (i, 0)
    vec = pl.BlockSpec((1, N), lambda i: (0, 0))
    in_specs, args = [], []
    for a in a_list:
        in_specs.append(pl.BlockSpec((tm, a.shape[1]), row))
        args.append(a)
    off = 0
    for a in a_list:
        ka = a.shape[1]
        in_specs.append(pl.BlockSpec((None, ka, N), functools.partial(lambda i, b: (layer, b, 0), b=off // ka)))
        args.append(w)
        off += ka
    in_specs += [pl.BlockSpec((tm, N), row), vec, vec, vec, vec]
    args += [x, gate, *next_norm]
    return pl.pallas_call(
        functools.partial(_proj_res_kernel, n_a=n_a),
        out_shape=[jax.ShapeDtypeStruct((S, N), F32), jax.ShapeDtypeStruct((S, N), BF16)],
        grid=(S // tm,),
        in_specs=in_specs,
        out_specs=[pl.BlockSpec((tm, N), row), pl.BlockSpec((tm, N), row)],
        compiler_params=_cparams(("parallel",)),
        name="proj_residual",
    )(*args)


def _ffn_kernel(h_ref, x_ref, gt_ref, wg_ref, wu_ref, wo_ref, g_ref, sh_ref, sc_ref, *rest, last):
    if last:
        o_ref, acc_ref = rest
    else:
        o_ref, hn_ref, acc_ref = rest
    k = pl.program_id(1)

    @pl.when(k == 0)
    def _():
        acc_ref[...] = jnp.zeros_like(acc_ref)

    h = h_ref[...]
    gate = _dot(h, wg_ref[...])
    up = _dot(h, wu_ref[...])
    act = (gate * _sigmoid(gate) * up).astype(BF16)
    acc_ref[...] += _dot(act, wo_ref[...])

    @pl.when(k == pl.num_programs(1) - 1)
    def _():
        xn = x_ref[...] + gt_ref[...] * acc_ref[...]
        hn = _norm_mod(xn, g_ref[...], sh_ref[...], sc_ref[...])
        if last:
            o_ref[...] = hn
        else:
            o_ref[...] = xn
            hn_ref[...] = hn.astype(BF16)


def ffn(h, x, gate, w_in, w_out, layer, norm, last):
    S, D = x.shape
    H = w_out.shape[1]
    tm, th = TILES["ffn"]
    nk = H // th
    row = lambda i, k: (i, 0)
    vec = pl.BlockSpec((1, D), lambda i, k: (0, 0))
    out_shape = [jax.ShapeDtypeStruct((S, D), F32)] + ([] if last else [jax.ShapeDtypeStruct((S, D), BF16)])
    return pl.pallas_call(
        functools.partial(_ffn_kernel, last=last),
        out_shape=out_shape,
        grid=(S // tm, nk),
        in_specs=[pl.BlockSpec((tm, D), row), pl.BlockSpec((tm, D), row), vec,
                  pl.BlockSpec((None, D, th), lambda i, k: (layer, 0, k)),
                  pl.BlockSpec((None, D, th), lambda i, k: (layer, 0, k + nk)),
                  pl.BlockSpec((None, th, D), lambda i, k: (layer, k, 0)),
                  vec, vec, vec],
        out_specs=[pl.BlockSpec((tm, D), row)] * len(out_shape),
        scratch_shapes=[pltpu.VMEM((tm, D), F32)],
        compiler_params=_cparams(("parallel", "arbitrary")),
        name="swiglu_ffn",
    )(h, x, gate, w_in, w_in, w_out, *norm)


def _segsum_heads(x, ones_bd):
    cols = [_dot_ones(x[:, c * LANES:(c + 1) * LANES], ones_bd) for c in range(x.shape[1] // LANES)]
    return jnp.concatenate(cols, axis=1)


def _head_ones():
    r = lax.broadcasted_iota(jnp.int32, (LANES, LANES), 0) // HEAD_DIM
    c = lax.broadcasted_iota(jnp.int32, (LANES, LANES), 1) // HEAD_DIM
    return jnp.where(r == c, 1.0, 0.0).astype(F32)


def _rwkv_prep_kernel(*refs, has_vres, tm):
    (p_ref, pp_ref, mu_ref, w0_ref, a0_ref, kk_w_ref, ka_ref, w2_ref, a2_ref, g2_ref) = refs[:10]
    if has_vres:
        vf_ref, v0_ref, v1_ref, v2_ref = refs[10:14]
        outs = refs[14:]
    else:
        outs = refs[10:]
    r_o, k_o, v_o, kk_o, b_o, cum_o, g_o = outs
    first = pl.program_id(0) == 0
    row = lax.broadcasted_iota(jnp.int32, (tm, 1), 0)

    def mixed(c0, c1):
        p = p_ref[:, c0:c1]
        prev_last = jnp.where(first, 0.0, pp_ref[7:8, c0:c1])
        sh = jnp.where(row == 0, prev_last, pltpu.roll(p, 1, axis=0))
        return p + (sh - p) * mu_ref[:, c0:c1]

    C = RWKV_WIDTH
    r = mixed(0, C)
    k = mixed(C, 2 * C)
    v = mixed(2 * C, 3 * C)
    wa = mixed(3 * C, 3 * C + LANES)
    gl = mixed(3 * C + LANES, RWKV_IN_PAD)

    lora = lambda act, w_ref: _mm3(_split(act), (w_ref[0], w_ref[1]))
    wlog = w0_ref[...] + lora(jnp.tanh(wa), w2_ref)
    xs = -wlog
    softplus = jnp.maximum(xs, 0.0) + jnp.log(1.0 + jnp.exp(-jnp.abs(xs)))
    w = -softplus - 0.5
    log_decay = -jnp.exp(w)
    a = _sigmoid(a0_ref[...] + lora(wa, a2_ref))
    g = lora(_sigmoid(gl), g2_ref)
    if has_vres:
        v = v + (vf_ref[...] - v) * _sigmoid(v0_ref[...] + lora(lora(v, v1_ref), v2_ref))

    kk = k * kk_w_ref[...]
    nrm = jnp.sqrt(_segsum_heads(kk * kk, _head_ones().astype(BF16)))
    kk = kk / jnp.maximum(nrm, 1e-12)
    k = k * (1.0 + (a - 1.0) * ka_ref[...])

    ti = lax.broadcasted_iota(jnp.int32, (tm, tm), 0)
    si = lax.broadcasted_iota(jnp.int32, (tm, tm), 1)
    tri = jnp.where((si <= ti) & (si // CHUNK == ti // CHUNK), 1.0, 0.0).astype(BF16)
    cum = sum(jnp.dot(tri, part, preferred_element_type=F32) for part in _split3(log_decay))

    r_o[...] = r
    k_o[...] = k
    v_o[...] = v
    kk_o[...] = kk
    b_o[...] = kk * a
    cum_o[...] = cum
    g_o[...] = g


def rwkv_prep(p, mu, w0, a0, k_k, k_a, w2p, a2p, g2p, vres):
    S = p.shape[0]
    tm = TILES["rwkv_prep"]
    C = RWKV_WIDTH
    row = lambda i: (i, 0)
    vec = lambda i: (0, 0)
    whole = lambda w: pl.BlockSpec(w.shape, lambda i: (0,) * w.ndim)
    in_specs = [pl.BlockSpec((tm, RWKV_IN_PAD), row),
                pl.BlockSpec((8, RWKV_IN_PAD), lambda i: (jnp.maximum(i * (tm // 8) - 1, 0), 0)),
                pl.BlockSpec((1, RWKV_IN_PAD), vec),
                pl.BlockSpec((1, C), vec), pl.BlockSpec((1, C), vec), pl.BlockSpec((1, C), vec),
                pl.BlockSpec((1, C), vec), whole(w2p), whole(a2p), whole(g2p)]
    args = [p, p, mu, w0, a0, k_k, k_a, w2p, a2p, g2p]
    if vres is not None:
        v_first, v0, v1p, v2p = vres
        in_specs += [pl.BlockSpec((tm, C), row), pl.BlockSpec((1, C), vec), whole(v1p), whole(v2p)]
        args += [v_first, v0, v1p, v2p]
    out = jax.ShapeDtypeStruct((S, C), F32)
    return pl.pallas_call(
        functools.partial(_rwkv_prep_kernel, has_vres=vres is not None, tm=tm),
        out_shape=[out] * 7,
        grid=(S // tm,),
        in_specs=in_specs,
        out_specs=[pl.BlockSpec((tm, C), row)] * 7,
        compiler_params=_cparams(("parallel",)),
        name="rwkv_prep",
    )(*args)


_NN = (((1,), (0,)), ((), ()))
_NT = (((1,), (1,)), ((), ()))
_TN = (((0,), (0,)), ((), ()))


def _split(x):
    hi = x.astype(BF16)
    return hi, (x - hi.astype(F32)).astype(BF16)


def _mm3(a, b, dims=_NN):
    f = lambda x, y: lax.dot_general(x, y, dims, preferred_element_type=F32)
    return f(a[0], b[0]) + (f(a[0], b[1]) + f(a[1], b[0]))


def _dot_ones(x, ones_bf16):
    hi = x.astype(BF16)
    r1 = x - hi.astype(F32)
    mid = r1.astype(BF16)
    lo = (r1 - mid.astype(F32)).astype(BF16)
    f = lambda t: jnp.dot(t, ones_bf16, preferred_element_type=F32)
    return f(hi) + (f(mid) + f(lo))


def _rwkv_chunk_kernel(r_ref, k_ref, v_ref, kk_ref, b_ref, cum_ref, g_ref, rk_ref, lnw_ref, lnb_ref,
                       o_ref, ht_ref, y_ref, *, n_chunks, n_pairs):
    L = CHUNK

    @pl.when(pl.program_id(1) == 0)
    def _():
        ht_ref[...] = jnp.zeros_like(ht_ref)

    row = lax.broadcasted_iota(jnp.int32, (L, LANES), 0)
    lane = lax.broadcasted_iota(jnp.int32, (L, LANES), 1)
    head0 = lane < HEAD_DIM
    ri = lax.broadcasted_iota(jnp.int32, (2 * L, 2 * L), 0)
    ci = lax.broadcasted_iota(jnp.int32, (2 * L, 2 * L), 1)
    same = (ri // L) == (ci // L)
    strict = same & ((ci % L) < (ri % L))
    incl = same & ((ci % L) <= (ri % L))

    def stack(x):
        return jnp.concatenate([jnp.where(head0, x, 0.0), jnp.where(head0, 0.0, x)], axis=0)

    items = [(c, p) for c in range(n_chunks) for p in range(n_pairs)]
    pre = [dict() for _ in items]
    for q, (c, p) in zip(pre, items):
        sl = (slice(c * L, (c + 1) * L), slice(p * LANES, (p + 1) * LANES))
        r, k, v, kk, b, cum = r_ref[sl], k_ref[sl], v_ref[sl], kk_ref[sl], b_ref[sl], cum_ref[sl]
        cum_excl = jnp.where(row == 0, 0.0, pltpu.roll(cum, 1, axis=0))
        e_pos = jnp.exp(cum)
        e_neg = jnp.exp(-cum)
        q["s_a"] = stack(-kk * jnp.exp(cum_excl))
        q["v"] = stack(v)
        q["s_r"] = stack(r * e_pos)
        sp_ar = _split(jnp.concatenate([q["s_a"], q["s_r"]], axis=0))
        q["bk"] = _split(jnp.concatenate([stack(b * e_neg), stack(k * e_neg)], axis=0))
        q["p_last"] = e_pos[L - 1:L, :]
        q["big"] = _mm3(sp_ar, q["bk"], _NT)
    for q in pre:
        big = q.pop("big")
        q["pw"] = jnp.where(strict, big[:2 * L, :2 * L], 0.0)
        q["a_ak"] = jnp.where(strict, big[:2 * L, 2 * L:], 0.0)
        q["rbk"] = _split(jnp.concatenate([jnp.where(incl, big[2 * L:, :2 * L], 0.0),
                                           jnp.where(incl, big[2 * L:, 2 * L:], 0.0)], axis=1))
    mm = lambda x, y: _mm3(_split(x), _split(y))
    mmb = lambda x, y: jnp.dot(x.astype(BF16), y.astype(BF16), preferred_element_type=F32)
    m0 = 8
    blk_of = lambda m: ((ri // m) == (ci // m))
    for q in pre:
        d1 = jnp.where(blk_of(m0), q["pw"], 0.0)
        q["d1"] = d1
        q["d2"] = mmb(d1, d1)
    for q in pre:
        d1, d2 = q.pop("d1"), q["d2"]
        q["n"] = d1 + d2 + mmb(d1, d2)
        q["d4"] = mmb(d2, d2)
        q.pop("d2")
    for q in pre:
        d4 = q.pop("d4")
        q["n"] = q["n"] + d4 + mmb(q["n"], d4)
    m = m0
    while m < L:
        for q in pre:
            e = jnp.where(blk_of(2 * m) & jnp.logical_not(blk_of(m)), q["pw"], 0.0)
            q["x"] = e + mmb(q["n"], e)
        for q in pre:
            x = q.pop("x")
            q["n"] = q["n"] + x + mmb(x, q["n"])
        m *= 2
    for q in pre:
        q.pop("pw")
        q["tmp"] = mm(q.pop("a_ak"), q["v"])
    for q in pre:
        x = jnp.concatenate([q.pop("s_a"), q.pop("tmp")], axis=1)
        q["wu"] = x + mm(q.pop("n"), x)
    for q in pre:
        wu = q.pop("wu")
        sp_w = _split(wu[:, :2 * L])
        sp_uv0 = _split(jnp.concatenate([wu[:, 2 * L:], q.pop("v")], axis=0))
        bk, rbk = q.pop("bk"), q.pop("rbk")
        q["mt"] = _split(_mm3(sp_w, (bk[0][:2 * L], bk[1][:2 * L]), _TN))
        q["gt"] = _mm3(sp_uv0, bk, _TN)
        q["rhat"] = _split(q.pop("s_r") + _mm3((rbk[0][:, :2 * L], rbk[1][:, :2 * L]), sp_w))
        q["yhat"] = _mm3(rbk, sp_uv0)

    hts = [ht_ref[p] for p in range(n_pairs)]
    for q, (c, p) in zip(pre, items):
        sp_h = _split(hts[p])
        y_s = _mm3(q["rhat"], sp_h, _NT) + q["yhat"]
        hts[p] = (hts[p] + _mm3(sp_h, q["mt"]) + q["gt"]) * q["p_last"]
        y_ref[c * L:(c + 1) * L, p * LANES:(p + 1) * LANES] = y_s[:L] + y_s[L:]
    for p in range(n_pairs):
        ht_ref[p] = hts[p]

    ones_bd = _head_ones().astype(BF16)
    inv_n = 1.0 / HEAD_DIM
    for p in range(n_pairs):
        cs = slice(p * LANES, (p + 1) * LANES)
        y = y_ref[:, cs]
        mean = _dot_ones(y, ones_bd) * inv_n
        yc = y - mean
        var = _dot_ones(yc * yc, ones_bd) * inv_n
        yn = yc * lax.rsqrt(var + GN_EPS) * lnw_ref[:, cs] + lnb_ref[:, cs]
        r, k, v = r_ref[:, cs], k_ref[:, cs], v_ref[:, cs]
        bonus = _dot_ones(r * k * rk_ref[:, cs], ones_bd)
        o_ref[:, cs] = ((yn + bonus * v) * g_ref[:, cs]).astype(o_ref.dtype)


def rwkv_chunk(r, k, v, kk, b, cum, g, r_k, ln_w, ln_b):
    S, C = r.shape
    n_chunks, n_pairs = TILES["rwkv_chunk"]
    tm = n_chunks * CHUNK
    tw = n_pairs * LANES
    blk = pl.BlockSpec((tm, tw), lambda h, c: (c, h))
    vec = pl.BlockSpec((1, tw), lambda h, c: (0, h))
    return pl.pallas_call(
        functools.partial(_rwkv_chunk_kernel, n_chunks=n_chunks, n_pairs=n_pairs),
        out_shape=jax.ShapeDtypeStruct((S, C), BF16),
        grid=(C // tw, S // tm),
        in_specs=[blk] * 7 + [vec] * 3,
        out_specs=blk,
        scratch_shapes=[pltpu.VMEM((n_pairs, LANES, LANES), F32), pltpu.VMEM((tm, tw), F32)],
        compiler_params=_cparams(("parallel", "arbitrary")),
        name="rwkv_chunk",
    )(r, k, v, kk, b, cum, g, r_k, ln_w, ln_b)


def _gelu_tanh(y):
    return 0.5 * y * (1.0 + jnp.tanh(math.sqrt(2.0 / math.pi) * (y + 0.044715 * (y * y * y))))


def _split3(x):
    hi = x.astype(BF16)
    r1 = x - hi.astype(F32)
    mid = r1.astype(BF16)
    return hi, mid, (r1 - mid.astype(F32)).astype(BF16)


def _cmul(ar, ai, br, bi):
    return ar * br - ai * bi, ar * bi + ai * br


def _s5_kernel(u_ref, kall_ref, wz_ref, call_ref, pw_ref, ct_ref, d_ref, o_ref, carry_ref, ys_ref, *, tm, sub):
    half = 8 * S5_STATE
    Q = S5_LAG
    nq = tm // Q

    @pl.when(pl.program_id(1) == 0)
    def _():
        carry_ref[...] = jnp.zeros_like(carry_ref)

    zin = jnp.concatenate([u_ref[pl.ds(Q - 1 - tau, nq, stride=Q), :] for tau in range(Q)], axis=1)
    z = jnp.dot(zin.astype(BF16), wz_ref[0], preferred_element_type=F32)

    xr, xi = z[:, :half], z[:, half:]
    crow = lax.broadcasted_iota(jnp.int32, (nq, 1), 0)
    for lvl in range(int(math.log2(nq))):
        off = 1 << lvl
        ar, ai = pw_ref[0, lvl:lvl + 1, :half], pw_ref[0, lvl:lvl + 1, half:]
        keep = crow >= off
        sr = jnp.where(keep, pltpu.roll(xr, off, axis=0), 0.0)
        si = jnp.where(keep, pltpu.roll(xi, off, axis=0), 0.0)
        xr, xi = xr + ar * sr - ai * si, xi + ar * si + ai * sr
    cr, ci = carry_ref[:, :half], carry_ref[:, half:]
    tr, ti = _cmul(ct_ref[0, :, :half], ct_ref[0, :, half:], cr, ci)
    xr, xi = xr + tr, xi + ti
    er = jnp.where(crow == 0, cr, pltpu.roll(xr, 1, axis=0))
    ei = jnp.where(crow == 0, ci, pltpu.roll(xi, 1, axis=0))
    carry_ref[:, :half] = xr[nq - 1:nq, :]
    carry_ref[:, half:] = xi[nq - 1:nq, :]

    ysf = jnp.dot(jnp.concatenate([er, ei], axis=1).astype(BF16), call_ref[0],
                  preferred_element_type=F32)
    for s in range(Q):
        ys_ref[pl.ds(s, nq, stride=Q), :] = ysf[:, s * LANES:(s + 1) * LANES]

    pos = lax.broadcasted_iota(jnp.int32, (sub, 1), 0) % Q
    for t0 in range(0, tm, sub):
        u = u_ref[t0:t0 + sub, :]
        lags = [u] + [jnp.where(pos >= tau, pltpu.roll(u, tau, axis=0), 0.0) for tau in range(1, Q)]
        y = jnp.dot(jnp.concatenate(lags, axis=1).astype(BF16), kall_ref[0], preferred_element_type=F32)
        o_ref[t0:t0 + sub, :] = _gelu_tanh(y + ys_ref[t0:t0 + sub, :] + d_ref[...] * u)


def s5_scan(p, col0, tables, d):
    kall, wz, call, pows, ctab = tables
    tm, sub = TILES["s5"]
    S = p.shape[0]
    nj = S5_WIDTH // LANES
    nst = 2 * 8 * S5_STATE
    nq = tm // S5_LAG
    ql = S5_LAG * LANES
    per_j = lambda *shape: pl.BlockSpec((1,) + shape, lambda j, i: (j,) + (0,) * len(shape))
    return pl.pallas_call(
        functools.partial(_s5_kernel, tm=tm, sub=sub),
        out_shape=jax.ShapeDtypeStruct((S, S5_WIDTH), F32),
        grid=(nj, S // tm),
        in_specs=[pl.BlockSpec((tm, LANES), lambda j, i: (i, col0 // LANES + j)),
                  per_j(ql, LANES), per_j(ql, nst), per_j(nst, ql),
                  per_j(pows.shape[1], nst), per_j(nq, nst),
                  pl.BlockSpec((1, LANES), lambda j, i: (0, j))],
        out_specs=pl.BlockSpec((tm, LANES), lambda j, i: (i, j)),
        scratch_shapes=[pltpu.VMEM((1, nst), F32), pltpu.VMEM((tm, LANES), F32)],
        compiler_params=_cparams(("parallel", "arbitrary")),
        name="s5_scan",
    )(p, kall, wz, call, pows, ctab, d)


def _s5_tables(lam_re, lam_im, log_step, b_re, b_im, c_re, c_im):
    G, P, GS = S5_WIDTH // S5_GROUP, S5_STATE, S5_GROUP
    nj = G // 8
    Q = S5_LAG
    nq = TILES["s5"][0] // Q
    lam = lax.complex(jnp.minimum(lam_re, -1e-4), lam_im)
    step = jnp.exp(log_step)[:, None]
    lam_dt = lam * step
    lam_bar = jnp.exp(lam_dt)
    b_bar = ((lam_bar - 1.0) / lam)[..., None] * lax.complex(b_re, b_im)

    def lam_pow(e):
        return jnp.exp(lam_dt[None] * e.astype(F32)[:, None, None])

    def pack(z):
        z = jnp.moveaxis(z.reshape(z.shape[0], nj, 8 * P), 1, 0)
        return jnp.concatenate([jnp.real(z), jnp.imag(z)], axis=-1).astype(F32)

    pows = pack(lam_pow(Q * 2 ** jnp.arange(int(math.log2(nq)))))
    ctab = pack(lam_pow(Q * (jnp.arange(nq) + 1)))

    eye8 = jnp.eye(8, dtype=F32)
    m = lam_pow(jnp.arange(Q))[..., None] * b_bar[None]
    def bd_in(x):
        return jnp.einsum('tjgph,gk->jtghkp', x.reshape(Q, nj, 8, P, GS), eye8).reshape(nj, Q * 8 * GS, 8 * P)
    wz = jnp.concatenate([bd_in(jnp.real(m)), bd_in(jnp.imag(m))], axis=-1).astype(F32)
    ls = lam_pow(jnp.arange(Q) + 1)[:, :, None, :]
    lr, li = jnp.real(ls), jnp.imag(ls)
    def bd_out(x):
        return jnp.einsum('tjghp,gk->jkptgh', x.reshape(Q, nj, 8, GS, P), eye8).reshape(nj, 8 * P, Q * 8 * GS)
    call = jnp.concatenate([bd_out(c_re[None] * lr - c_im[None] * li),
                            -bd_out(c_re[None] * li + c_im[None] * lr)], axis=1).astype(F32)
    ktab = (jnp.einsum('ghp,tgpk->tghk', c_re, jnp.real(m), precision=HI)
            - jnp.einsum('ghp,tgpk->tghk', c_im, jnp.imag(m), precision=HI))
    kall = jnp.einsum('tjghk,gm->jtgkmh', ktab.reshape(Q, nj, 8, GS, GS), eye8).reshape(nj, Q * 8 * GS, 8 * GS)
    return kall.astype(BF16), wz.astype(BF16), call.astype(BF16), pows, ctab


def _glu_kernel(y_ref, yc_ref, w_ref, b_ref, o_ref, a_ref):
    @pl.when(pl.program_id(1) == 0)
    def _():
        a_ref[...] = y_ref[...].astype(BF16)

    z = _dot(a_ref[...], w_ref[...]) + b_ref[...]
    o_ref[...] = (yc_ref[...] * _sigmoid(z)).astype(o_ref.dtype)


def glu(y, w, layer, b):
    S, C = y.shape
    tm, tn = TILES["glu"]
    return pl.pallas_call(
        _glu_kernel,
        out_shape=jax.ShapeDtypeStruct((S, C), BF16),
        grid=(S // tm, C // tn),
        in_specs=[pl.BlockSpec((tm, C), lambda i, j: (i, 0)), pl.BlockSpec((tm, tn), lambda i, j: (i, j)),
                  pl.BlockSpec((None, C, tn), lambda i, j: (layer, 0, j)),
                  pl.BlockSpec((1, tn), lambda i, j: (0, j))],
        out_specs=pl.BlockSpec((tm, tn), lambda i, j: (i, j)),
        scratch_shapes=[pltpu.VMEM((tm, C), BF16)],
        compiler_params=_cparams(("parallel", "arbitrary")),
        name="s5_glu",
    )(y, y, w, b)


def _attn_block(qs, ks, vs, bias, head0):
    zero = jnp.zeros((), qs.dtype)
    pvs, dens, lses = [], [], []
    for hmask in (head0, jnp.logical_not(head0)):
        s = _dot_nt(jnp.where(hmask, qs, zero), ks) + bias
        m = jnp.max(s, axis=-1, keepdims=True)
        pexp = jnp.exp(s - m)
        den = jnp.sum(pexp, axis=-1, keepdims=True)
        pvs.append(_dot(pexp.astype(vs.dtype), vs))
        dens.append(den)
        lses.append(m + jnp.log(den))
    out = jnp.where(head0, pvs[0], pvs[1]) / jnp.where(head0, dens[0], dens[1])
    return out, jnp.where(head0, lses[0], lses[1])


def _attn_kernel(*refs, dilations, tile):
    nb = len(dilations)
    in_refs = [refs[5 * b:5 * b + 5] for b in range(nb)]
    out_ref, o_s, l_s = refs[5 * nb:]
    B = ATTN_BLK
    first_tile = pl.program_id(1) == 0
    head0 = lax.broadcasted_iota(jnp.int32, (B, LANES), 1) < HEAD_DIM
    qi = lax.broadcasted_iota(jnp.int32, (B, 2 * B), 0)
    kj = lax.broadcasted_iota(jnp.int32, (B, 2 * B), 1)
    in_band = (kj >= qi) & (kj <= qi + B)
    band = jnp.where(in_band, 0.0, -jnp.inf)
    band_first = jnp.where(in_band & (kj >= jnp.where(first_tile, B, 0)), 0.0, -jnp.inf)
    scale = HEAD_DIM ** -0.5

    for b, d in enumerate(dilations):
        q_ref, kc_ref, kp_ref, vc_ref, vp_ref = in_refs[b]
        n_blk = tile // d // B

        def block(r, row0, first_blk, b=b, d=d, refs_=in_refs[b]):
            q_ref, kc_ref, kp_ref, vc_ref, vp_ref = refs_
            qs = q_ref[r, pl.ds(row0, B), :] * scale
            if first_blk:
                ks = jnp.concatenate([kp_ref[r], kc_ref[r, :B, :]], axis=0)
                vs = jnp.concatenate([vp_ref[r], vc_ref[r, :B, :]], axis=0)
            else:
                ks = kc_ref[r, pl.ds(row0 - B, 2 * B), :]
                vs = vc_ref[r, pl.ds(row0 - B, 2 * B), :]
            o, lse = _attn_block(qs, ks, vs, band_first if first_blk else band, head0)
            dst = pl.ds(row0 * d + r, B, stride=d) if d > 1 else pl.ds(row0, B)
            o_s[b, dst, :] = o
            l_s[b, dst, :] = lse

        for r in range(d):
            for qb in range(n_blk):
                block(r, qb * B, qb == 0)

    rc = 2 * B

    def merge(c, carry):
        sl = pl.ds(pl.multiple_of(c * rc, rc), rc)
        ls_ = [l_s[b, sl, :] for b in range(nb)]
        m = functools.reduce(jnp.maximum, ls_)
        es = [jnp.exp(l - m) for l in ls_]
        num = functools.reduce(lambda x, y: x + y, [e * o_s[b, sl, :] for b, e in enumerate(es)])
        den = functools.reduce(lambda x, y: x + y, es)
        out_ref[sl, :] = (num / den).astype(out_ref.dtype)
        return carry
    lax.fori_loop(0, tile // rc, merge, 0)


def dilated_attention(qkvs, tile=ATTN_TILE):
    dilations = tuple(t.shape[0] for t in qkvs)
    S = qkvs[0].shape[0] * qkvs[0].shape[1]
    nh = D_MODEL // LANES
    in_specs, args = [], []
    for t, d in zip(qkvs, dilations):
        rows = tile // d
        rpb = rows // ATTN_BLK

        def cur(part, rows=rows, d=d):
            return pl.BlockSpec((d, rows, LANES), lambda h, i: (0, i, part * nh + h))

        def prev(part, rpb=rpb, d=d):
            return pl.BlockSpec((d, ATTN_BLK, LANES),
                                lambda h, i: (0, jnp.maximum(i * rpb - 1, 0), part * nh + h))

        in_specs += [cur(0), cur(1), prev(1), cur(2), prev(2)]
        args += [t] * 5
    nb = len(dilations)
    return pl.pallas_call(
        functools.partial(_attn_kernel, dilations=dilations, tile=tile),
        out_shape=jax.ShapeDtypeStruct((S, D_MODEL), BF16),
        grid=(nh, S // tile),
        in_specs=in_specs,
        out_specs=pl.BlockSpec((tile, LANES), lambda h, i: (i, h)),
        scratch_shapes=[pltpu.VMEM((nb, tile, LANES), F32), pltpu.VMEM((nb, tile, LANES), F32)],
        compiler_params=_cparams(("parallel", "arbitrary")),
        name="dilated_attention",
    )(*args)


def _rotary_tables(S):
    half = ROT_DIM // 2
    inv = ROPE_THETA ** (-jnp.arange(half, dtype=F32) * 2.0 / ROT_DIM)
    ang = jnp.arange(S, dtype=F32)[:, None] * inv[None, :]
    cos, sin = jnp.cos(ang), jnp.sin(ang)
    rest = HEAD_DIM - ROT_DIM
    ones = jnp.ones((S, rest), F32)
    zeros = jnp.zeros((S, rest), F32)
    zh = jnp.zeros((S, half), F32)
    cos_t = jnp.concatenate([cos, cos, ones], axis=1)
    sin_a = jnp.concatenate([-sin, zh, zeros], axis=1)
    sin_b = jnp.concatenate([zh, sin, zeros], axis=1)
    rep = LANES // HEAD_DIM
    return tuple(jnp.tile(t, (1, rep)) for t in (cos_t, sin_a, sin_b))


def _pad_rows(w, n):
    return jnp.pad(w, ((0, n - w.shape[0]), (0, 0)))


def _split_w(w):
    hi = w.astype(BF16)
    return jnp.stack([hi, (w - hi.astype(F32)).astype(BF16)])


def kernel(x, c, ada_w, ada_b, norm_mix_g, norm_ffn_g, hyb_w_in, hyb_w_out, rwkv_mu, rwkv_w0, rwkv_w2, rwkv_a0, rwkv_a2, rwkv_g2, rwkv_k_k, rwkv_k_a, rwkv_r_k, rwkv_ln_w, rwkv_ln_b, rwkv_v0, rwkv_v1, rwkv_v2, s5_lam_re, s5_lam_im, s5_log_step, s5_b_re, s5_b_im, s5_c_re, s5_c_im, s5_d, s5_glu_w, s5_glu_b, attn_w_qkv, attn_w_o, ffn_w_in, ffn_w_out, final_norm_g):
    B, S, D = x.shape
    assert B == 1 and D == D_MODEL and S % 2048 == 0
    C = RWKV_WIDTH
    xs = x.reshape(S, D)
    mod = ada_modulation(c, ada_w, ada_b)
    rot_tables = _rotary_tables(S)
    dilations = tuple(d for (_, d) in DILATED_BRANCHES)
    row = lambda t: t.reshape(1, -1)
    mods = [[mod[i][:, q * D:(q + 1) * D] for q in range(6)] for i in range(DEPTH)]
    mix_norm = lambda i: (row(norm_mix_g[i]), mods[i][0], mods[i][1])
    ffn_norm = lambda i: (row(norm_ffn_g[i]), mods[i][3], mods[i][4])

    gpad = RWKV_IN_PAD - RWKV_IN
    n_even = hyb_w_in.shape[0]
    w_hyb_in = jnp.concatenate([hyb_w_in[:, :, :RWKV_IN], jnp.zeros((n_even, D, gpad), F32), hyb_w_in[:, :, RWKV_IN:],
                                jnp.zeros((n_even, D, EVEN_IN_TILED - EVEN_IN_PAD), F32)], axis=2).astype(BF16)
    w_hyb_out, w_glu = hyb_w_out.astype(BF16), s5_glu_w.astype(BF16)
    w_qkv, w_attn_o = attn_w_qkv.astype(BF16), attn_w_o.astype(BF16)
    w_ffn_in, w_ffn_out = ffn_w_in.astype(BF16), ffn_w_out.astype(BF16)

    v_first = None
    h = norm_mod_rows(xs, mix_norm(0))
    for i in range(DEPTH):
        gt_mix, gt_ffn = mods[i][2], mods[i][5]
        j = i // 2
        if i % 2 == 0:
            mu = jnp.concatenate([rwkv_mu[j], jnp.zeros((gpad,), F32)]).reshape(1, RWKV_IN_PAD)
            p = in_proj(h, w_hyb_in, j)

            w2p = _split_w(jnp.concatenate([rwkv_w2[j], jnp.zeros((LORA_A, C), F32)], axis=0))
            a2p = _split_w(jnp.concatenate([jnp.zeros((LORA_W, C), F32), rwkv_a2[j]], axis=0))
            g2p = _split_w(_pad_rows(rwkv_g2[j], 2 * LANES))
            vres = None
            if j > 0:
                v1p = _split_w(jnp.pad(rwkv_v1[j - 1], ((0, 0), (0, LANES - LORA_V))))
                v2p = _split_w(_pad_rows(rwkv_v2[j - 1], LANES))
                vres = (v_first, row(rwkv_v0[j - 1]), v1p, v2p)
            r, k, v, kk, b, cum, g = rwkv_prep(p, mu, row(rwkv_w0[j]), row(rwkv_a0[j]), row(rwkv_k_k[j]),
                                               row(rwkv_k_a[j]), w2p, a2p, g2p, vres)
            if j == 0:
                v_first = v
            y_rwkv = rwkv_chunk(r, k, v, kk, b, cum, g, row(rwkv_r_k[j]), row(rwkv_ln_w[j]),
                                row(rwkv_ln_b[j]))

            s5_tabs = _s5_tables(s5_lam_re[j], s5_lam_im[j], s5_log_step[j], s5_b_re[j], s5_b_im[j],
                                 s5_c_re[j], s5_c_im[j])
            y_s5 = s5_scan(p, RWKV_IN_PAD, s5_tabs, row(s5_d[j]))
            y_s5 = glu(y_s5, w_glu, j, row(s5_glu_b[j]))
            xs, h = proj_residual([y_rwkv, y_s5], w_hyb_out, j, xs, gt_mix, ffn_norm(i))
        else:
            qkvs = qkv_proj(h, w_qkv, j, rot_tables, 2 * D, dilations)
            o = dilated_attention(qkvs)
            xs, h = proj_residual([o], w_attn_o, j, xs, gt_mix, ffn_norm(i))
        if i < DEPTH - 1:
            xs, h = ffn(h, xs, gt_ffn, w_ffn_in, w_ffn_out, i, mix_norm(i + 1), last=False)
        else:
            zero = jnp.zeros((1, D), F32)
            (xs,) = ffn(h, xs, gt_ffn, w_ffn_in, w_ffn_out, i, (row(final_norm_g), zero, zero), last=True)
    return xs.reshape(B, S, D)
```

```python
import functools
import math

import jax
import jax.numpy as jnp
from jax import lax
from jax.experimental import pallas as pl
from jax.experimental.pallas import tpu as pltpu

F32 = jnp.float32
BF16 = jnp.bfloat16
HI = lax.Precision.HIGHEST

D_MODEL = 2048
DEPTH = 4
HEAD_DIM = 64
RWKV_WIDTH = 1024
LORA_W = 64
LORA_A = 64
LORA_V = 32
LORA_G = 160
RWKV_IN = 3 * RWKV_WIDTH + LORA_W + LORA_A + LORA_G
RWKV_IN_PAD = 3456
S5_WIDTH = 1024
S5_GROUP = 16
S5_STATE = 64
S5_LAG = 8
EVEN_IN_PAD = RWKV_IN_PAD + S5_WIDTH
EVEN_IN_TILED = 4608
ATTN_HEADS = 32
ROT_DIM = 16
ROPE_THETA = 500000.0
DILATED_BRANCHES = ((128, 1), (512, 4), (2048, 16))
ATTN_BLK = 128
ATTN_TILE = 2048
FFN_HIDDEN = 5632
RMS_EPS = 1e-6
GN_EPS = 64e-5

LANES = 128
CHUNK = 64
VMEM_LIMIT = 48 * 1024 * 1024

TILES = {
    "ada": 1024,
    "norm_rows": 512,
    "in_proj": (1024, 768),
    "qkv_proj": (2048, 512),
    "out_proj": 512,
    "ffn": (512, 512),
    "glu": (512, 512),
    "rwkv_prep": 256,
    "rwkv_chunk": (4, 4),
    "s5": (2048, 256),
}


def _cparams(sem):
    return pltpu.CompilerParams(dimension_semantics=sem, vmem_limit_bytes=VMEM_LIMIT)


def _sigmoid(x):
    return 1.0 / (1.0 + jnp.exp(-x))


def _dot(a, b, prec=None):
    return jnp.dot(a, b, preferred_element_type=F32, precision=prec)


def _dot_nt(a, b, prec=None):
    return lax.dot_general(a, b, (((1,), (1,)), ((), ())), preferred_element_type=F32, precision=prec)


def _dot_tn(a, b, prec=None):
    return lax.dot_general(a, b, (((0,), (0,)), ((), ())), preferred_element_type=F32, precision=prec)


def _norm_mod(x, g, shift, scale):
    ms = jnp.mean(x * x, axis=-1, keepdims=True)
    return (x * lax.rsqrt(ms + RMS_EPS) * g) * (1.0 + scale) + shift


def _ada_kernel(c_ref, w_ref, b_ref, o_ref):
    c = c_ref[...]
    s = c * _sigmoid(c)
    o_ref[0] = jnp.sum(s * w_ref[0], axis=0, keepdims=True) + b_ref[0]


def ada_modulation(c, ada_w, ada_b):
    L, D, N = ada_w.shape
    tn = TILES["ada"]
    return pl.pallas_call(
        _ada_kernel,
        out_shape=jax.ShapeDtypeStruct((L, 1, N), F32),
        grid=(L, N // tn),
        in_specs=[pl.BlockSpec((D, 1), lambda l, j: (0, 0)),
                  pl.BlockSpec((1, D, tn), lambda l, j: (l, 0, j)),
                  pl.BlockSpec((1, 1, tn), lambda l, j: (l, 0, j))],
        out_specs=pl.BlockSpec((1, 1, tn), lambda l, j: (l, 0, j)),
        compiler_params=_cparams(("parallel", "parallel")),
        name="ada_modulation",
    )(c.reshape(D, 1), ada_w, ada_b.reshape(L, 1, N))


def _norm_mod_kernel(x_ref, g_ref, sh_ref, sc_ref, h_ref):
    h_ref[...] = _norm_mod(x_ref[...], g_ref[...], sh_ref[...], sc_ref[...]).astype(BF16)


def norm_mod_rows(x, norm):
    S, D = x.shape
    tm = TILES["norm_rows"]
    vec = pl.BlockSpec((1, D), lambda i: (0, 0))
    return pl.pallas_call(
        _norm_mod_kernel,
        out_shape=jax.ShapeDtypeStruct((S, D), BF16),
        grid=(S // tm,),
        in_specs=[pl.BlockSpec((tm, D), lambda i: (i, 0)), vec, vec, vec],
        out_specs=pl.BlockSpec((tm, D), lambda i: (i, 0)),
        compiler_params=_cparams(("parallel",)),
        name="norm_mod_rows",
    )(x, *norm)


def _matmul_kernel(h_ref, w_ref, o_ref):
    o_ref[...] = _dot(h_ref[...], w_ref[...]).astype(o_ref.dtype)


def in_proj(h, w, layer):
    S, D = h.shape
    N = w.shape[2]
    tm, tn = TILES["in_proj"]
    return pl.pallas_call(
        _matmul_kernel,
        out_shape=jax.ShapeDtypeStruct((S, N), F32),
        grid=(S // tm, N // tn),
        in_specs=[pl.BlockSpec((tm, D), lambda i, j: (i, 0)),
                  pl.BlockSpec((None, D, tn), lambda i, j: (layer, 0, j))],
        out_specs=pl.BlockSpec((tm, tn), lambda i, j: (i, j)),
        compiler_params=_cparams(("parallel", "parallel")),
        name="in_proj",
    )(h, w)


def _qkv_kernel(h_ref, w_ref, cos_ref, sa_ref, sb_ref, *rest, n_rot_blocks, dilations):
    o_refs, t_ref = rest[:len(dilations)], rest[len(dilations)]
    j = pl.program_id(1)
    n_lane_blocks, tm, _ = t_ref.shape
    acc = _dot(h_ref[...], w_ref[...])

    @pl.when(j < n_rot_blocks)
    def _():
        cos, sa, sb = cos_ref[...], sa_ref[...], sb_ref[...]
        half = ROT_DIM // 2
        for c in range(n_lane_blocks):
            t = acc[:, c * LANES:(c + 1) * LANES]
            t_ref[c] = t * cos + pltpu.roll(t, LANES - half, axis=1) * sa + pltpu.roll(t, half, axis=1) * sb

    @pl.when(j >= n_rot_blocks)
    def _():
        for c in range(n_lane_blocks):
            t_ref[c] = acc[:, c * LANES:(c + 1) * LANES]

    for d, o_ref in zip(dilations, o_refs):
        for c in range(n_lane_blocks):
            cs = slice(c * LANES, (c + 1) * LANES)
            if d == 1:
                o_ref[0, :, cs] = t_ref[c].astype(o_ref.dtype)
                continue
            for r in range(d):
                o_ref[r, :, cs] = t_ref[c, pl.ds(r, tm // d, stride=d), :].astype(o_ref.dtype)


def qkv_proj(h, w, layer, rot_tables, n_rot_cols, dilations):
    S, D = h.shape
    N = w.shape[2]
    tm, tn = TILES["qkv_proj"]
    return pl.pallas_call(
        functools.partial(_qkv_kernel, n_rot_blocks=n_rot_cols // tn, dilations=dilations),
        out_shape=[jax.ShapeDtypeStruct((d, S // d, N), BF16) for d in dilations],
        grid=(S // tm, N // tn),
        in_specs=[pl.BlockSpec((tm, D), lambda i, j: (i, 0)),
                  pl.BlockSpec((None, D, tn), lambda i, j: (layer, 0, j))]
                 + [pl.BlockSpec((tm, LANES), lambda i, j: (i, 0))] * 3,
        out_specs=[pl.BlockSpec((d, tm // d, tn), lambda i, j: (0, i, j)) for d in dilations],
        scratch_shapes=[pltpu.VMEM((tn // LANES, tm, LANES), F32)],
        compiler_params=_cparams(("parallel", "parallel")),
        name="qkv_proj",
    )(h, w, *rot_tables)


def _proj_res_kernel(*refs, n_a):
    a_refs, w_refs = refs[:n_a], refs[n_a:2 * n_a]
    x_ref, gt_ref, g_ref, sh_ref, sc_ref, o_ref, h_ref = refs[2 * n_a:]
    acc = _dot(a_refs[0][...], w_refs[0][...])
    for a_ref, w_ref in zip(a_refs[1:], w_refs[1:]):
        acc = acc + _dot(a_ref[...], w_ref[...])
    xn = x_ref[...] + gt_ref[...] * acc
    o_ref[...] = xn
    h_ref[...] = _norm_mod(xn, g_ref[...], sh_ref[...], sc_ref[...]).astype(BF16)


def proj_residual(a_list, w, layer, x, gate, next_norm):
    S, N = x.shape
    tm = TILES["out_proj"]
    n_a = len(a_list)
    row = lambda i: (i, 0)
    vec = pl.BlockSpec((1, N), lambda i: (0, 0))
    in_specs, args = [], []
    for a in a_list:
        in_specs.append(pl.BlockSpec((tm, a.shape[1]), row))
        args.append(a)
    off = 0
    for a in a_list:
        ka = a.shape[1]
        in_specs.append(pl.BlockSpec((None, ka, N), functools.partial(lambda i, b: (layer, b, 0), b=off // ka)))
        args.append(w)
        off += ka
    in_specs += [pl.BlockSpec((tm, N), row), vec, vec, vec, vec]
    args += [x, gate, *next_norm]
    return pl.pallas_call(
        functools.partial(_proj_res_kernel, n_a=n_a),
        out_shape=[jax.ShapeDtypeStruct((S, N), F32), jax.ShapeDtypeStruct((S, N), BF16)],
        grid=(S // tm,),
        in_specs=in_specs,
        out_specs=[pl.BlockSpec((tm, N), row), pl.BlockSpec((tm, N), row)],
        compiler_params=_cparams(("parallel",)),
        name="proj_residual",
    )(*args)


def _ffn_kernel(h_ref, x_ref, gt_ref, wg_ref, wu_ref, wo_ref, g_ref, sh_ref, sc_ref, *rest, last):
    if last:
        o_ref, acc_ref = rest
    else:
        o_ref, hn_ref, acc_ref = rest
    k = pl.program_id(1)

    @pl.when(k == 0)
    def _():
        acc_ref[...] = jnp.zeros_like(acc_ref)

    h = h_ref[...]
    gate = _dot(h, wg_ref[...])
    up = _dot(h, wu_ref[...])
    act = (gate * _sigmoid(gate) * up).astype(BF16)
    acc_ref[...] += _dot(act, wo_ref[...])

    @pl.when(k == pl.num_programs(1) - 1)
    def _():
        xn = x_ref[...] + gt_ref[...] * acc_ref[...]
        hn = _norm_mod(xn, g_ref[...], sh_ref[...], sc_ref[...])
        if last:
            o_ref[...] = hn
        else:
            o_ref[...] = xn
            hn_ref[...] = hn.astype(BF16)


def ffn(h, x, gate, w_in, w_out, layer, norm, last):
    S, D = x.shape
    H = w_out.shape[1]
    tm, th = TILES["ffn"]
    nk = H // th
    row = lambda i, k: (i, 0)
    vec = pl.BlockSpec((1, D), lambda i, k: (0, 0))
    out_shape = [jax.ShapeDtypeStruct((S, D), F32)] + ([] if last else [jax.ShapeDtypeStruct((S, D), BF16)])
    return pl.pallas_call(
        functools.partial(_ffn_kernel, last=last),
        out_shape=out_shape,
        grid=(S // tm, nk),
        in_specs=[pl.BlockSpec((tm, D), row), pl.BlockSpec((tm, D), row), vec,
                  pl.BlockSpec((None, D, th), lambda i, k: (layer, 0, k)),
                  pl.BlockSpec((None, D, th), lambda i, k: (layer, 0, k + nk)),
                  pl.BlockSpec((None, th, D), lambda i, k: (layer, k, 0)),
                  vec, vec, vec],
        out_specs=[pl.BlockSpec((tm, D), row)] * len(out_shape),
        scratch_shapes=[pltpu.VMEM((tm, D), F32)],
        compiler_params=_cparams(("parallel", "arbitrary")),
        name="swiglu_ffn",
    )(h, x, gate, w_in, w_in, w_out, *norm)


def _segsum_heads(x, ones_bd):
    cols = [_dot_ones(x[:, c * LANES:(c + 1) * LANES], ones_bd) for c in range(x.shape[1] // LANES)]
    return jnp.concatenate(cols, axis=1)


def _head_ones():
    r = lax.broadcasted_iota(jnp.int32, (LANES, LANES), 0) // HEAD_DIM
    c = lax.broadcasted_iota(jnp.int32, (LANES, LANES), 1) // HEAD_DIM
    return jnp.where(r == c, 1.0, 0.0).astype(F32)


def _rwkv_prep_kernel(*refs, has_vres, tm):
    (p_ref, pp_ref, mu_ref, w0_ref, a0_ref, kk_w_ref, ka_ref, w2_ref, a2_ref, g2_ref) = refs[:10]
    if has_vres:
        vf_ref, v0_ref, v1_ref, v2_ref = refs[10:14]
        outs = refs[14:]
    else:
        outs = refs[10:]
    r_o, k_o, v_o, kk_o, b_o, cum_o, g_o = outs
    first = pl.program_id(0) == 0
    row = lax.broadcasted_iota(jnp.int32, (tm, 1), 0)

    def mixed(c0, c1):
        p = p_ref[:, c0:c1]
        prev_last = jnp.where(first, 0.0, pp_ref[7:8, c0:c1])
        sh = jnp.where(row == 0, prev_last, pltpu.roll(p, 1, axis=0))
        return p + (sh - p) * mu_ref[:, c0:c1]

    C = RWKV_WIDTH
    r = mixed(0, C)
    k = mixed(C, 2 * C)
    v = mixed(2 * C, 3 * C)
    wa = mixed(3 * C, 3 * C + LANES)
    gl = mixed(3 * C + LANES, RWKV_IN_PAD)

    lora = lambda act, w_ref: _mm3(_split(act), (w_ref[0], w_ref[1]))
    wlog = w0_ref[...] + lora(jnp.tanh(wa), w2_ref)
    xs = -wlog
    softplus = jnp.maximum(xs, 0.0) + jnp.log(1.0 + jnp.exp(-jnp.abs(xs)))
    w = -softplus - 0.5
    log_decay = -jnp.exp(w)
    a = _sigmoid(a0_ref[...] + lora(wa, a2_ref))
    g = lora(_sigmoid(gl), g2_ref)
    if has_vres:
        v = v + (vf_ref[...] - v) * _sigmoid(v0_ref[...] + lora(lora(v, v1_ref), v2_ref))

    kk = k * kk_w_ref[...]
    nrm = jnp.sqrt(_segsum_heads(kk * kk, _head_ones().astype(BF16)))
    kk = kk / jnp.maximum(nrm, 1e-12)
    k = k * (1.0 + (a - 1.0) * ka_ref[...])

    ti = lax.broadcasted_iota(jnp.int32, (tm, tm), 0)
    si = lax.broadcasted_iota(jnp.int32, (tm, tm), 1)
    tri = jnp.where((si <= ti) & (si // CHUNK == ti // CHUNK), 1.0, 0.0).astype(BF16)
    cum = sum(jnp.dot(tri, part, preferred_element_type=F32) for part in _split3(log_decay))

    r_o[...] = r
    k_o[...] = k
    v_o[...] = v
    kk_o[...] = kk
    b_o[...] = kk * a
    cum_o[...] = cum
    g_o[...] = g


def rwkv_prep(p, mu, w0, a0, k_k, k_a, w2p, a2p, g2p, vres):
    S = p.shape[0]
    tm = TILES["rwkv_prep"]
    C = RWKV_WIDTH
    row = lambda i: (i, 0)
    vec = lambda i: (0, 0)
    whole = lambda w: pl.BlockSpec(w.shape, lambda i: (0,) * w.ndim)
    in_specs = [pl.BlockSpec((tm, RWKV_IN_PAD), row),
                pl.BlockSpec((8, RWKV_IN_PAD), lambda i: (jnp.maximum(i * (tm // 8) - 1, 0), 0)),
                pl.BlockSpec((1, RWKV_IN_PAD), vec),
                pl.BlockSpec((1, C), vec), pl.BlockSpec((1, C), vec), pl.BlockSpec((1, C), vec),
                pl.BlockSpec((1, C), vec), whole(w2p), whole(a2p), whole(g2p)]
    args = [p, p, mu, w0, a0, k_k, k_a, w2p, a2p, g2p]
    if vres is not None:
        v_first, v0, v1p, v2p = vres
        in_specs += [pl.BlockSpec((tm, C), row), pl.BlockSpec((1, C), vec), whole(v1p), whole(v2p)]
        args += [v_first, v0, v1p, v2p]
    out = jax.ShapeDtypeStruct((S, C), F32)
    return pl.pallas_call(
        functools.partial(_rwkv_prep_kernel, has_vres=vres is not None, tm=tm),
        out_shape=[out] * 7,
        grid=(S // tm,),
        in_specs=in_specs,
        out_specs=[pl.BlockSpec((tm, C), row)] * 7,
        compiler_params=_cparams(("parallel",)),
        name="rwkv_prep",
    )(*args)


_NN = (((1,), (0,)), ((), ()))
_NT = (((1,), (1,)), ((), ()))
_TN = (((0,), (0,)), ((), ()))


def _split(x):
    hi = x.astype(BF16)
    return hi, (x - hi.astype(F32)).astype(BF16)


def _mm3(a, b, dims=_NN):
    f = lambda x, y: lax.dot_general(x, y, dims, preferred_element_type=F32)
    return f(a[0], b[0]) + (f(a[0], b[1]) + f(a[1], b[0]))


def _dot_ones(x, ones_bf16):
    hi = x.astype(BF16)
    r1 = x - hi.astype(F32)
    mid = r1.astype(BF16)
    lo = (r1 - mid.astype(F32)).astype(BF16)
    f = lambda t: jnp.dot(t, ones_bf16, preferred_element_type=F32)
    return f(hi) + (f(mid) + f(lo))


def _rwkv_chunk_kernel(r_ref, k_ref, v_ref, kk_ref, b_ref, cum_ref, g_ref, rk_ref, lnw_ref, lnb_ref,
                       o_ref, ht_ref, y_ref, *, n_chunks, n_pairs):
    L = CHUNK

    @pl.when(pl.program_id(1) == 0)
    def _():
        ht_ref[...] = jnp.zeros_like(ht_ref)

    row = lax.broadcasted_iota(jnp.int32, (L, LANES), 0)
    lane = lax.broadcasted_iota(jnp.int32, (L, LANES), 1)
    head0 = lane < HEAD_DIM
    ri = lax.broadcasted_iota(jnp.int32, (2 * L, 2 * L), 0)
    ci = lax.broadcasted_iota(jnp.int32, (2 * L, 2 * L), 1)
    same = (ri // L) == (ci // L)
    strict = same & ((ci % L) < (ri % L))
    incl = same & ((ci % L) <= (ri % L))

    def stack(x):
        return jnp.concatenate([jnp.where(head0, x, 0.0), jnp.where(head0, 0.0, x)], axis=0)

    items = [(c, p) for c in range(n_chunks) for p in range(n_pairs)]
    pre = [dict() for _ in items]
    for q, (c, p) in zip(pre, items):
        sl = (slice(c * L, (c + 1) * L), slice(p * LANES, (p + 1) * LANES))
        r, k, v, kk, b, cum = r_ref[sl], k_ref[sl], v_ref[sl], kk_ref[sl], b_ref[sl], cum_ref[sl]
        cum_excl = jnp.where(row == 0, 0.0, pltpu.roll(cum, 1, axis=0))
        e_pos = jnp.exp(cum)
        e_neg = jnp.exp(-cum)
        q["s_a"] = stack(-kk * jnp.exp(cum_excl))
        q["v"] = stack(v)
        q["s_r"] = stack(r * e_pos)
        sp_ar = _split(jnp.concatenate([q["s_a"], q["s_r"]], axis=0))
        q["bk"] = _split(jnp.concatenate([stack(b * e_neg), stack(k * e_neg)], axis=0))
        q["p_last"] = e_pos[L - 1:L, :]
        q["big"] = _mm3(sp_ar, q["bk"], _NT)
    for q in pre:
        big = q.pop("big")
        q["pw"] = jnp.where(strict, big[:2 * L, :2 * L], 0.0)
        q["a_ak"] = jnp.where(strict, big[:2 * L, 2 * L:], 0.0)
        q["rbk"] = _split(jnp.concatenate([jnp.where(incl, big[2 * L:, :2 * L], 0.0),
                                           jnp.where(incl, big[2 * L:, 2 * L:], 0.0)], axis=1))
    mm = lambda x, y: _mm3(_split(x), _split(y))
    mmb = lambda x, y: jnp.dot(x.astype(BF16), y.astype(BF16), preferred_element_type=F32)
    m0 = 8
    blk_of = lambda m: ((ri // m) == (ci // m))
    for q in pre:
        d1 = jnp.where(blk_of(m0), q["pw"], 0.0)
        q["d1"] = d1
        q["d2"] = mmb(d1, d1)
    for q in pre:
        d1, d2 = q.pop("d1"), q["d2"]
        q["n"] = d1 + d2 + mmb(d1, d2)
        q["d4"] = mmb(d2, d2)
        q.pop("d2")
    for q in pre:
        d4 = q.pop("d4")
        q["n"] = q["n"] + d4 + mmb(q["n"], d4)
    m = m0
    while m < L:
        for q in pre:
            e = jnp.where(blk_of(2 * m) & jnp.logical_not(blk_of(m)), q["pw"], 0.0)
            q["x"] = e + mmb(q["n"], e)
        for q in pre:
            x = q.pop("x")
            q["n"] = q["n"] + x + mmb(x, q["n"])
        m *= 2
    for q in pre:
        q.pop("pw")
        q["tmp"] = mm(q.pop("a_ak"), q["v"])
    for q in pre:
        x = jnp.concatenate([q.pop("s_a"), q.pop("tmp")], axis=1)
        q["wu"] = x + mm(q.pop("n"), x)
    for q in pre:
        wu = q.pop("wu")
        sp_w = _split(wu[:, :2 * L])
        sp_uv0 = _split(jnp.concatenate([wu[:, 2 * L:], q.pop("v")], axis=0))
        bk, rbk = q.pop("bk"), q.pop("rbk")
        q["mt"] = _split(_mm3(sp_w, (bk[0][:2 * L], bk[1][:2 * L]), _TN))
        q["gt"] = _mm3(sp_uv0, bk, _TN)
        q["rhat"] = _split(q.pop("s_r") + _mm3((rbk[0][:, :2 * L], rbk[1][:, :2 * L]), sp_w))
        q["yhat"] = _mm3(rbk, sp_uv0)

    hts = [ht_ref[p] for p in range(n_pairs)]
    for q, (c, p) in zip(pre, items):
        sp_h = _split(hts[p])
        y_s = _mm3(q["rhat"], sp_h, _NT) + q["yhat"]
        hts[p] = (hts[p] + _mm3(sp_h, q["mt"]) + q["gt"]) * q["p_last"]
        y_ref[c * L:(c + 1) * L, p * LANES:(p + 1) * LANES] = y_s[:L] + y_s[L:]
    for p in range(n_pairs):
        ht_ref[p] = hts[p]

    ones_bd = _head_ones().astype(BF16)
    inv_n = 1.0 / HEAD_DIM
    for p in range(n_pairs):
        cs = slice(p * LANES, (p + 1) * LANES)
        y = y_ref[:, cs]
        mean = _dot_ones(y, ones_bd) * inv_n
        yc = y - mean
        var = _dot_ones(yc * yc, ones_bd) * inv_n
        yn = yc * lax.rsqrt(var + GN_EPS) * lnw_ref[:, cs] + lnb_ref[:, cs]
        r, k, v = r_ref[:, cs], k_ref[:, cs], v_ref[:, cs]
        bonus = _dot_ones(r * k * rk_ref[:, cs], ones_bd)
        o_ref[:, cs] = ((yn + bonus * v) * g_ref[:, cs]).astype(o_ref.dtype)


def rwkv_chunk(r, k, v, kk, b, cum, g, r_k, ln_w, ln_b):
    S, C = r.shape
    n_chunks, n_pairs = TILES["rwkv_chunk"]
    tm = n_chunks * CHUNK
    tw = n_pairs * LANES
    blk = pl.BlockSpec((tm, tw), lambda h, c: (c, h))
    vec = pl.BlockSpec((1, tw), lambda h, c: (0, h))
    return pl.pallas_call(
        functools.partial(_rwkv_chunk_kernel, n_chunks=n_chunks, n_pairs=n_pairs),
        out_shape=jax.ShapeDtypeStruct((S, C), BF16),
        grid=(C // tw, S // tm),
        in_specs=[blk] * 7 + [vec] * 3,
        out_specs=blk,
        scratch_shapes=[pltpu.VMEM((n_pairs, LANES, LANES), F32), pltpu.VMEM((tm, tw), F32)],
        compiler_params=_cparams(("parallel", "arbitrary")),
        name="rwkv_chunk",
    )(r, k, v, kk, b, cum, g, r_k, ln_w, ln_b)


def _gelu_tanh(y):
    return 0.5 * y * (1.0 + jnp.tanh(math.sqrt(2.0 / math.pi) * (y + 0.044715 * (y * y * y))))


def _split3(x):
    hi = x.astype(BF16)
    r1 = x - hi.astype(F32)
    mid = r1.astype(BF16)
    return hi, mid, (r1 - mid.astype(F32)).astype(BF16)


def _cmul(ar, ai, br, bi):
    return ar * br - ai * bi, ar * bi + ai * br


def _s5_kernel(u_ref, kall_ref, wz_ref, call_ref, pw_ref, ct_ref, d_ref, o_ref, carry_ref, ys_ref, *, tm, sub):
    half = 8 * S5_STATE
    Q = S5_LAG
    nq = tm // Q

    @pl.when(pl.program_id(1) == 0)
    def _():
        carry_ref[...] = jnp.zeros_like(carry_ref)

    zin = jnp.concatenate([u_ref[pl.ds(Q - 1 - tau, nq, stride=Q), :] for tau in range(Q)], axis=1)
    z = jnp.dot(zin.astype(BF16), wz_ref[0], preferred_element_type=F32)

    xr, xi = z[:, :half], z[:, half:]
    crow = lax.broadcasted_iota(jnp.int32, (nq, 1), 0)
    for lvl in range(int(math.log2(nq))):
        off = 1 << lvl
        ar, ai = pw_ref[0, lvl:lvl + 1, :half], pw_ref[0, lvl:lvl + 1, half:]
        keep = crow >= off
        sr = jnp.where(keep, pltpu.roll(xr, off, axis=0), 0.0)
        si = jnp.where(keep, pltpu.roll(xi, off, axis=0), 0.0)
        xr, xi = xr + ar * sr - ai * si, xi + ar * si + ai * sr
    cr, ci = carry_ref[:, :half], carry_ref[:, half:]
    tr, ti = _cmul(ct_ref[0, :, :half], ct_ref[0, :, half:], cr, ci)
    xr, xi = xr + tr, xi + ti
    er = jnp.where(crow == 0, cr, pltpu.roll(xr, 1, axis=0))
    ei = jnp.where(crow == 0, ci, pltpu.roll(xi, 1, axis=0))
    carry_ref[:, :half] = xr[nq - 1:nq, :]
    carry_ref[:, half:] = xi[nq - 1:nq, :]

    ysf = jnp.dot(jnp.concatenate([er, ei], axis=1).astype(BF16), call_ref[0],
                  preferred_element_type=F32)
    for s in range(Q):
        ys_ref[pl.ds(s, nq, stride=Q), :] = ysf[:, s * LANES:(s + 1) * LANES]

    pos = lax.broadcasted_iota(jnp.int32, (sub, 1), 0) % Q
    for t0 in range(0, tm, sub):
        u = u_ref[t0:t0 + sub, :]
        lags = [u] + [jnp.where(pos >= tau, pltpu.roll(u, tau, axis=0), 0.0) for tau in range(1, Q)]
        y = jnp.dot(jnp.concatenate(lags, axis=1).astype(BF16), kall_ref[0], preferred_element_type=F32)
        o_ref[t0:t0 + sub, :] = _gelu_tanh(y + ys_ref[t0:t0 + sub, :] + d_ref[...] * u)


def s5_scan(p, col0, tables, d):
    kall, wz, call, pows, ctab = tables
    tm, sub = TILES["s5"]
    S = p.shape[0]
    nj = S5_WIDTH // LANES
    nst = 2 * 8 * S5_STATE
    nq = tm // S5_LAG
    ql = S5_LAG * LANES
    per_j = lambda *shape: pl.BlockSpec((1,) + shape, lambda j, i: (j,) + (0,) * len(shape))
    return pl.pallas_call(
        functools.partial(_s5_kernel, tm=tm, sub=sub),
        out_shape=jax.ShapeDtypeStruct((S, S5_WIDTH), F32),
        grid=(nj, S // tm),
        in_specs=[pl.BlockSpec((tm, LANES), lambda j, i: (i, col0 // LANES + j)),
                  per_j(ql, LANES), per_j(ql, nst), per_j(nst, ql),
                  per_j(pows.shape[1], nst), per_j(nq, nst),
                  pl.BlockSpec((1, LANES), lambda j, i: (0, j))],
        out_specs=pl.BlockSpec((tm, LANES), lambda j, i: (i, j)),
        scratch_shapes=[pltpu.VMEM((1, nst), F32), pltpu.VMEM((tm, LANES), F32)],
        compiler_params=_cparams(("parallel", "arbitrary")),
        name="s5_scan",
    )(p, kall, wz, call, pows, ctab, d)


def _s5_tables(lam_re, lam_im, log_step, b_re, b_im, c_re, c_im):
    G, P, GS = S5_WIDTH // S5_GROUP, S5_STATE, S5_GROUP
    nj = G // 8
    Q = S5_LAG
    nq = TILES["s5"][0] // Q
    lam = lax.complex(jnp.minimum(lam_re, -1e-4), lam_im)
    step = jnp.exp(log_step)[:, None]
    lam_dt = lam * step
    lam_bar = jnp.exp(lam_dt)
    b_bar = ((lam_bar - 1.0) / lam)[..., None] * lax.complex(b_re, b_im)

    def lam_pow(e):
        return jnp.exp(lam_dt[None] * e.astype(F32)[:, None, None])

    def pack(z):
        z = jnp.moveaxis(z.reshape(z.shape[0], nj, 8 * P), 1, 0)
        return jnp.concatenate([jnp.real(z), jnp.imag(z)], axis=-1).astype(F32)

    pows = pack(lam_pow(Q * 2 ** jnp.arange(int(math.log2(nq)))))
    ctab = pack(lam_pow(Q * (jnp.arange(nq) + 1)))

    eye8 = jnp.eye(8, dtype=F32)
    ql = Q * 8 * GS
    grp_of_ql = (jnp.arange(ql) // GS) % 8
    grp_of_state = jnp.arange(8 * P) // P
    m = lam_pow(jnp.arange(Q))[..., None] * b_bar[None]
    def bd_in(x):
        rows = jnp.transpose(x.reshape(Q, nj, 8, P, GS), (1, 0, 2, 4, 3)).reshape(nj, ql, P)
        return jnp.where(grp_of_ql[:, None] == grp_of_state[None, :], jnp.tile(rows, (1, 1, 8)), 0.0)
    wz = jnp.concatenate([bd_in(jnp.real(m)), bd_in(jnp.imag(m))], axis=-1).astype(F32)
    ls = lam_pow(jnp.arange(Q) + 1)[:, :, None, :]
    lr, li = jnp.real(ls), jnp.imag(ls)
    def bd_out(x):
        cols = jnp.transpose(x.reshape(Q, nj, 8, GS, P), (1, 4, 0, 2, 3)).reshape(nj, P, ql)
        return jnp.where(grp_of_state[:, None] == grp_of_ql[None, :], jnp.tile(cols, (1, 8, 1)), 0.0)
    call = jnp.concatenate([bd_out(c_re[None] * lr - c_im[None] * li),
                            -bd_out(c_re[None] * li + c_im[None] * lr)], axis=1).astype(F32)
    ktab = (jnp.einsum('ghp,tgpk->tghk', c_re, jnp.real(m), precision=HI)
            - jnp.einsum('ghp,tgpk->tghk', c_im, jnp.imag(m), precision=HI))
    kall = jnp.einsum('tjghk,gm->jtgkmh', ktab.reshape(Q, nj, 8, GS, GS), eye8).reshape(nj, Q * 8 * GS, 8 * GS)
    return kall.astype(BF16), wz.astype(BF16), call.astype(BF16), pows, ctab


def _glu_kernel(y_ref, yc_ref, w_ref, b_ref, o_ref, a_ref):
    @pl.when(pl.program_id(1) == 0)
    def _():
        a_ref[...] = y_ref[...].astype(BF16)

    z = _dot(a_ref[...], w_ref[...]) + b_ref[...]
    o_ref[...] = (yc_ref[...] * _sigmoid(z)).astype(o_ref.dtype)


def glu(y, w, layer, b):
    S, C = y.shape
    tm, tn = TILES["glu"]
    return pl.pallas_call(
        _glu_kernel,
        out_shape=jax.ShapeDtypeStruct((S, C), BF16),
        grid=(S // tm, C // tn),
        in_specs=[pl.BlockSpec((tm, C), lambda i, j: (i, 0)), pl.BlockSpec((tm, tn), lambda i, j: (i, j)),
                  pl.BlockSpec((None, C, tn), lambda i, j: (layer, 0, j)),
                  pl.BlockSpec((1, tn), lambda i, j: (0, j))],
        out_specs=pl.BlockSpec((tm, tn), lambda i, j: (i, j)),
        scratch_shapes=[pltpu.VMEM((tm, C), BF16)],
        compiler_params=_cparams(("parallel", "arbitrary")),
        name="s5_glu",
    )(y, y, w, b)


def _attn_block(qs, ks, vs, bias, head0):
    zero = jnp.zeros((), qs.dtype)
    pvs, dens, lses = [], [], []
    for hmask in (head0, jnp.logical_not(head0)):
        s = _dot_nt(jnp.where(hmask, qs, zero), ks) + bias
        m = jnp.max(s, axis=-1, keepdims=True)
        pexp = jnp.exp(s - m)
        den = jnp.sum(pexp, axis=-1, keepdims=True)
        pvs.append(_dot(pexp.astype(vs.dtype), vs))
        dens.append(den)
        lses.append(m + jnp.log(den))
    out = jnp.where(head0, pvs[0], pvs[1]) / jnp.where(head0, dens[0], dens[1])
    return out, jnp.where(head0, lses[0], lses[1])


def _attn_kernel(*refs, dilations, tile):
    nb = len(dilations)
    in_refs = [refs[5 * b:5 * b + 5] for b in range(nb)]
    out_ref, o_s, l_s = refs[5 * nb:]
    B = ATTN_BLK
    first_tile = pl.program_id(1) == 0
    head0 = lax.broadcasted_iota(jnp.int32, (B, LANES), 1) < HEAD_DIM
    qi = lax.broadcasted_iota(jnp.int32, (B, 2 * B), 0)
    kj = lax.broadcasted_iota(jnp.int32, (B, 2 * B), 1)
    in_band = (kj >= qi) & (kj <= qi + B)
    band = jnp.where(in_band, 0.0, -jnp.inf)
    band_first = jnp.where(in_band & (kj >= jnp.where(first_tile, B, 0)), 0.0, -jnp.inf)
    scale = HEAD_DIM ** -0.5

    for b, d in enumerate(dilations):
        q_ref, kc_ref, kp_ref, vc_ref, vp_ref = in_refs[b]
        n_blk = tile // d // B

        def block(r, row0, first_blk, b=b, d=d, refs_=in_refs[b]):
            q_ref, kc_ref, kp_ref, vc_ref, vp_ref = refs_
            qs = q_ref[r, pl.ds(row0, B), :] * scale
            if first_blk:
                ks = jnp.concatenate([kp_ref[r], kc_ref[r, :B, :]], axis=0)
                vs = jnp.concatenate([vp_ref[r], vc_ref[r, :B, :]], axis=0)
            else:
                ks = kc_ref[r, pl.ds(row0 - B, 2 * B), :]
                vs = vc_ref[r, pl.ds(row0 - B, 2 * B), :]
            o, lse = _attn_block(qs, ks, vs, band_first if first_blk else band, head0)
            dst = pl.ds(row0 * d + r, B, stride=d) if d > 1 else pl.ds(row0, B)
            o_s[b, dst, :] = o
            l_s[b, dst, :] = lse

        for r in range(d):
            for qb in range(n_blk):
                block(r, qb * B, qb == 0)

    rc = 2 * B

    def merge(c, carry):
        sl = pl.ds(pl.multiple_of(c * rc, rc), rc)
        ls_ = [l_s[b, sl, :] for b in range(nb)]
        m = functools.reduce(jnp.maximum, ls_)
        es = [jnp.exp(l - m) for l in ls_]
        num = functools.reduce(lambda x, y: x + y, [e * o_s[b, sl, :] for b, e in enumerate(es)])
        den = functools.reduce(lambda x, y: x + y, es)
        out_ref[sl, :] = (num / den).astype(out_ref.dtype)
        return carry
    lax.fori_loop(0, tile // rc, merge, 0)


def dilated_attention(qkvs, tile=ATTN_TILE):
    dilations = tuple(t.shape[0] for t in qkvs)
    S = qkvs[0].shape[0] * qkvs[0].shape[1]
    nh = D_MODEL // LANES
    in_specs, args = [], []
    for t, d in zip(qkvs, dilations):
        rows = tile // d
        rpb = rows // ATTN_BLK

        def cur(part, rows=rows, d=d):
            return pl.BlockSpec((d, rows, LANES), lambda h, i: (0, i, part * nh + h))

        def prev(part, rpb=rpb, d=d):
            return pl.BlockSpec((d, ATTN_BLK, LANES),
                                lambda h, i: (0, jnp.maximum(i * rpb - 1, 0), part * nh + h))

        in_specs += [cur(0), cur(1), prev(1), cur(2), prev(2)]
        args += [t] * 5
    nb = len(dilations)
    return pl.pallas_call(
        functools.partial(_attn_kernel, dilations=dilations, tile=tile),
        out_shape=jax.ShapeDtypeStruct((S, D_MODEL), BF16),
        grid=(nh, S // tile),
        in_specs=in_specs,
        out_specs=pl.BlockSpec((tile, LANES), lambda h, i: (i, h)),
        scratch_shapes=[pltpu.VMEM((nb, tile, LANES), F32), pltpu.VMEM((nb, tile, LANES), F32)],
        compiler_params=_cparams(("parallel", "arbitrary")),
        name="dilated_attention",
    )(*args)


def _rotary_tables(S):
    half = ROT_DIM // 2
    inv = ROPE_THETA ** (-jnp.arange(half, dtype=F32) * 2.0 / ROT_DIM)
    ang = jnp.arange(S, dtype=F32)[:, None] * inv[None, :]
    cos, sin = jnp.cos(ang), jnp.sin(ang)
    rest = HEAD_DIM - ROT_DIM
    ones = jnp.ones((S, rest), F32)
    zeros = jnp.zeros((S, rest), F32)
    zh = jnp.zeros((S, half), F32)
    cos_t = jnp.concatenate([cos, cos, ones], axis=1)
    sin_a = jnp.concatenate([-sin, zh, zeros], axis=1)
    sin_b = jnp.concatenate([zh, sin, zeros], axis=1)
    rep = LANES // HEAD_DIM
    return tuple(jnp.tile(t, (1, rep)) for t in (cos_t, sin_a, sin_b))


def _pad_rows(w, n):
    return jnp.pad(w, ((0, n - w.shape[0]), (0, 0)))


def _split_w(w):
    hi = w.astype(BF16)
    return jnp.stack([hi, (w - hi.astype(F32)).astype(BF16)])


def kernel(x, c, ada_w, ada_b, norm_mix_g, norm_ffn_g, hyb_w_in, hyb_w_out, rwkv_mu, rwkv_w0, rwkv_w2, rwkv_a0, rwkv_a2, rwkv_g2, rwkv_k_k, rwkv_k_a, rwkv_r_k, rwkv_ln_w, rwkv_ln_b, rwkv_v0, rwkv_v1, rwkv_v2, s5_lam_re, s5_lam_im, s5_log_step, s5_b_re, s5_b_im, s5_c_re, s5_c_im, s5_d, s5_glu_w, s5_glu_b, attn_w_qkv, attn_w_o, ffn_w_in, ffn_w_out, final_norm_g):
    B, S, D = x.shape
    assert B == 1 and D == D_MODEL and S % 2048 == 0
    C = RWKV_WIDTH
    xs = x.reshape(S, D)
    mod = ada_modulation(c, ada_w, ada_b)
    rot_tables = _rotary_tables(S)
    dilations = tuple(d for (_, d) in DILATED_BRANCHES)
    row = lambda t: t.reshape(1, -1)
    mods = [[mod[i][:, q * D:(q + 1) * D] for q in range(6)] for i in range(DEPTH)]
    mix_norm = lambda i: (row(norm_mix_g[i]), mods[i][0], mods[i][1])
    ffn_norm = lambda i: (row(norm_ffn_g[i]), mods[i][3], mods[i][4])

    gpad = RWKV_IN_PAD - RWKV_IN
    n_even = hyb_w_in.shape[0]
    w_hyb_in = jnp.concatenate([hyb_w_in[:, :, :RWKV_IN], jnp.zeros((n_even, D, gpad), F32), hyb_w_in[:, :, RWKV_IN:],
                                jnp.zeros((n_even, D, EVEN_IN_TILED - EVEN_IN_PAD), F32)], axis=2).astype(BF16)
    w_hyb_out, w_glu = hyb_w_out.astype(BF16), s5_glu_w.astype(BF16)
    w_qkv, w_attn_o = attn_w_qkv.astype(BF16), attn_w_o.astype(BF16)
    w_ffn_in, w_ffn_out = ffn_w_in.astype(BF16), ffn_w_out.astype(BF16)

    v_first = None
    h = norm_mod_rows(xs, mix_norm(0))
    for i in range(DEPTH):
        gt_mix, gt_ffn = mods[i][2], mods[i][5]
        j = i // 2
        if i % 2 == 0:
            mu = jnp.concatenate([rwkv_mu[j], jnp.zeros((gpad,), F32)]).reshape(1, RWKV_IN_PAD)
            p = in_proj(h, w_hyb_in, j)

            w2p = _split_w(jnp.concatenate([rwkv_w2[j], jnp.zeros((LORA_A, C), F32)], axis=0))
            a2p = _split_w(jnp.concatenate([jnp.zeros((LORA_W, C), F32), rwkv_a2[j]], axis=0))
            g2p = _split_w(_pad_rows(rwkv_g2[j], 2 * LANES))
            vres = None
            if j > 0:
                v1p = _split_w(jnp.pad(rwkv_v1[j - 1], ((0, 0), (0, LANES - LORA_V))))
                v2p = _split_w(_pad_rows(rwkv_v2[j - 1], LANES))
                vres = (v_first, row(rwkv_v0[j - 1]), v1p, v2p)
            r, k, v, kk, b, cum, g = rwkv_prep(p, mu, row(rwkv_w0[j]), row(rwkv_a0[j]), row(rwkv_k_k[j]),
                                               row(rwkv_k_a[j]), w2p, a2p, g2p, vres)
            if j == 0:
                v_first = v
            y_rwkv = rwkv_chunk(r, k, v, kk, b, cum, g, row(rwkv_r_k[j]), row(rwkv_ln_w[j]),
                                row(rwkv_ln_b[j]))

            s5_tabs = _s5_tables(s5_lam_re[j], s5_lam_im[j], s5_log_step[j], s5_b_re[j], s5_b_im[j],
                                 s5_c_re[j], s5_c_im[j])
            y_s5 = s5_scan(p, RWKV_IN_PAD, s5_tabs, row(s5_d[j]))
            y_s5 = glu(y_s5, w_glu, j, row(s5_glu_b[j]))
            xs, h = proj_residual([y_rwkv, y_s5], w_hyb_out, j, xs, gt_mix, ffn_norm(i))
        else:
            qkvs = qkv_proj(h, w_qkv, j, rot_tables, 2 * D, dilations)
            o = dilated_attention(qkvs)
            xs, h = proj_residual([o], w_attn_o, j, xs, gt_mix, ffn_norm(i))
        if i < DEPTH - 1:
            xs, h = ffn(h, xs, gt_ffn, w_ffn_in, w_ffn_out, i, mix_norm(i + 1), last=False)
        else:
            zero = jnp.zeros((1, D), F32)
            (xs,) = ffn(h, xs, gt_ffn, w_ffn_in, w_ffn_out, i, (row(final_norm_g), zero, zero), last=True)
    return xs.reshape(B, S, D)
```

```python
import functools
import math

import jax
import jax.numpy as jnp
from jax import lax
from jax.experimental import pallas as pl
from jax.experimental.pallas import tpu as pltpu

F32 = jnp.float32
BF16 = jnp.bfloat16
HI = lax.Precision.HIGHEST

D_MODEL = 2048
DEPTH = 4
HEAD_DIM = 64
RWKV_WIDTH = 1024
LORA_W = 64
LORA_A = 64
LORA_V = 32
LORA_G = 160
RWKV_IN = 3 * RWKV_WIDTH + LORA_W + LORA_A + LORA_G
RWKV_IN_PAD = 3456
S5_WIDTH = 1024
S5_GROUP = 16
S5_STATE = 64
S5_LAG = 8
EVEN_IN_PAD = RWKV_IN_PAD + S5_WIDTH
EVEN_IN_TILED = 4608
ATTN_HEADS = 32
ROT_DIM = 16
ROPE_THETA = 500000.0
DILATED_BRANCHES = ((128, 1), (512, 4), (2048, 16))
ATTN_BLK = 128
ATTN_TILE = 2048
FFN_HIDDEN = 5632
RMS_EPS = 1e-6
GN_EPS = 64e-5

LANES = 128
CHUNK = 64
VMEM_LIMIT = 48 * 1024 * 1024

TILES = {
    "ada": 1024,
    "norm_rows": 512,
    "in_proj": (1024, 768),
    "qkv_proj": (2048, 512),
    "out_proj": 512,
    "ffn": (512, 512),
    "glu": (512, 512),
    "rwkv_prep": 256,
    "rwkv_chunk": (4, 4),
    "s5": (2048, 256),
}


def _cparams(sem):
    return pltpu.CompilerParams(dimension_semantics=sem, vmem_limit_bytes=VMEM_LIMIT)


def _sigmoid(x):
    return 1.0 / (1.0 + jnp.exp(-x))


def _dot(a, b):
    return jnp.dot(a, b, preferred_element_type=F32)


def _dot_nt(a, b):
    return lax.dot_general(a, b, (((1,), (1,)), ((), ())), preferred_element_type=F32)


def _norm_mod(x, g, shift, scale):
    ms = jnp.mean(x * x, axis=-1, keepdims=True)
    return (x * lax.rsqrt(ms + RMS_EPS) * g) * (1.0 + scale) + shift


def _ada_kernel(c_ref, w_ref, b_ref, o_ref):
    c = c_ref[...]
    s = c * _sigmoid(c)
    o_ref[0] = jnp.sum(s * w_ref[0], axis=0, keepdims=True) + b_ref[0]


def ada_modulation(c, ada_w, ada_b):
    L, D, N = ada_w.shape
    tn = TILES["ada"]
    return pl.pallas_call(
        _ada_kernel,
        out_shape=jax.ShapeDtypeStruct((L, 1, N), F32),
        grid=(L, N // tn),
        in_specs=[pl.BlockSpec((D, 1), lambda l, j: (0, 0)),
                  pl.BlockSpec((1, D, tn), lambda l, j: (l, 0, j)),
                  pl.BlockSpec((1, 1, tn), lambda l, j: (l, 0, j))],
        out_specs=pl.BlockSpec((1, 1, tn), lambda l, j: (l, 0, j)),
        compiler_params=_cparams(("parallel", "parallel")),
        name="ada_modulation",
    )(c.reshape(D, 1), ada_w, ada_b.reshape(L, 1, N))


def _norm_mod_kernel(x_ref, g_ref, sh_ref, sc_ref, h_ref):
    h_ref[...] = _norm_mod(x_ref[...], g_ref[...], sh_ref[...], sc_ref[...]).astype(BF16)


def norm_mod_rows(x, norm):
    S, D = x.shape
    tm = TILES["norm_rows"]
    vec = pl.BlockSpec((1, D), lambda i: (0, 0))
    return pl.pallas_call(
        _norm_mod_kernel,
        out_shape=jax.ShapeDtypeStruct((S, D), BF16),
        grid=(S // tm,),
        in_specs=[pl.BlockSpec((tm, D), lambda i: (i, 0)), vec, vec, vec],
        out_specs=pl.BlockSpec((tm, D), lambda i: (i, 0)),
        compiler_params=_cparams(("parallel",)),
        name="norm_mod_rows",
    )(x, *norm)


def _matmul_kernel(h_ref, w_ref, o_ref):
    o_ref[...] = _dot(h_ref[...], w_ref[...]).astype(o_ref.dtype)


def in_proj(h, w, layer):
    S, D = h.shape
    N = w.shape[2]
    tm, tn = TILES["in_proj"]
    return pl.pallas_call(
        _matmul_kernel,
        out_shape=jax.ShapeDtypeStruct((S, N), F32),
        grid=(S // tm, N // tn),
        in_specs=[pl.BlockSpec((tm, D), lambda i, j: (i, 0)),
                  pl.BlockSpec((None, D, tn), lambda i, j: (layer, 0, j))],
        out_specs=pl.BlockSpec((tm, tn), lambda i, j: (i, j)),
        compiler_params=_cparams(("parallel", "parallel")),
        name="in_proj",
    )(h, w)


def _qkv_kernel(h_ref, w_ref, cos_ref, sa_ref, sb_ref, *rest, n_rot_blocks, dilations):
    o_refs, t_ref = rest[:len(dilations)], rest[len(dilations)]
    j = pl.program_id(1)
    n_lane_blocks, tm, _ = t_ref.shape
    acc = _dot(h_ref[...], w_ref[...])

    @pl.when(j < n_rot_blocks)
    def _():
        cos, sa, sb = cos_ref[...], sa_ref[...], sb_ref[...]
        half = ROT_DIM // 2
        for c in range(n_lane_blocks):
            t = acc[:, c * LANES:(c + 1) * LANES]
            t_ref[c] = t * cos + pltpu.roll(t, LANES - half, axis=1) * sa + pltpu.roll(t, half, axis=1) * sb

    @pl.when(j >= n_rot_blocks)
    def _():
        for c in range(n_lane_blocks):
            t_ref[c] = acc[:, c * LANES:(c + 1) * LANES]

    for d, o_ref in zip(dilations, o_refs):
        for c in range(n_lane_blocks):
            cs = slice(c * LANES, (c + 1) * LANES)
            if d == 1:
                o_ref[0, :, cs] = t_ref[c].astype(o_ref.dtype)
                continue
            for r in range(d):
                o_ref[r, :, cs] = t_ref[c, pl.ds(r, tm // d, stride=d), :].astype(o_ref.dtype)


def qkv_proj(h, w, layer, rot_tables, n_rot_cols, dilations):
    S, D = h.shape
    N = w.shape[2]
    tm, tn = TILES["qkv_proj"]
    return pl.pallas_call(
        functools.partial(_qkv_kernel, n_rot_blocks=n_rot_cols // tn, dilations=dilations),
        out_shape=[jax.ShapeDtypeStruct((d, S // d, N), BF16) for d in dilations],
        grid=(S // tm, N // tn),
        in_specs=[pl.BlockSpec((tm, D), lambda i, j: (i, 0)),
                  pl.BlockSpec((None, D, tn), lambda i, j: (layer, 0, j))]
                 + [pl.BlockSpec((tm, LANES), lambda i, j: (i, 0))] * 3,
        out_specs=[pl.BlockSpec((d, tm // d, tn), lambda i, j: (0, i, j)) for d in dilations],
        scratch_shapes=[pltpu.VMEM((tn // LANES, tm, LANES), F32)],
        compiler_params=_cparams(("parallel", "parallel")),
        name="qkv_proj",
    )(h, w, *rot_tables)


def _proj_res_kernel(*refs, n_a):
    a_refs, w_refs = refs[:n_a], refs[n_a:2 * n_a]
    x_ref, gt_ref, g_ref, sh_ref, sc_ref, o_ref, h_ref = refs[2 * n_a:]
    acc = _dot(a_refs[0][...], w_refs[0][...])
    for a_ref, w_ref in zip(a_refs[1:], w_refs[1:]):
        acc = acc + _dot(a_ref[...], w_ref[...])
    xn = x_ref[...] + gt_ref[...] * acc
    o_ref[...] = xn
    h_ref[...] = _norm_mod(xn, g_ref[...], sh_ref[...], sc_ref[...]).astype(BF16)


def proj_residual(a_list, w, layer, x, gate, next_norm):
    S, N = x.shape
    tm = TILES["out_proj"]
    n_a = len(a_list)
    row = lambda i: (i, 0)
    vec = pl.BlockSpec((1, N), lambda i: (0, 0))
    in_specs, args = [], []
    for a in a_list:
        in_specs.append(pl.BlockSpec((tm, a.shape[1]), row))
        args.append(a)
    off = 0
    for a in a_list:
        ka = a.shape[1]
        in_specs.append(pl.BlockSpec((None, ka, N), functools.partial(lambda i, b: (layer, b, 0), b=off // ka)))
        args.append(w)
        off += ka
    in_specs += [pl.BlockSpec((tm, N), row), vec, vec, vec, vec]
    args += [x, gate, *next_norm]
    return pl.pallas_call(
        functools.partial(_proj_res_kernel, n_a=n_a),
        out_shape=[jax.ShapeDtypeStruct((S, N), F32), jax.ShapeDtypeStruct((S, N), BF16)],
        grid=(S // tm,),
        in_specs=in_specs,
        out_specs=[pl.BlockSpec((tm, N), row), pl.BlockSpec((tm, N), row)],
        compiler_params=_cparams(("parallel",)),
        name="proj_residual",
    )(*args)


def _ffn_kernel(h_ref, x_ref, gt_ref, wg_ref, wu_ref, wo_ref, g_ref, sh_ref, sc_ref, *rest, last):
    if last:
        o_ref, acc_ref = rest
    else:
        o_ref, hn_ref, acc_ref = rest
    k = pl.program_id(1)

    @pl.when(k == 0)
    def _():
        acc_ref[...] = jnp.zeros_like(acc_ref)

    h = h_ref[...]
    gate = _dot(h, wg_ref[...])
    up = _dot(h, wu_ref[...])
    act = (gate * _sigmoid(gate) * up).astype(BF16)
    acc_ref[...] += _dot(act, wo_ref[...])

    @pl.when(k == pl.num_programs(1) - 1)
    def _():
        xn = x_ref[...] + gt_ref[...] * acc_ref[...]
        hn = _norm_mod(xn, g_ref[...], sh_ref[...], sc_ref[...])
        if last:
            o_ref[...] = hn
        else:
            o_ref[...] = xn
            hn_ref[...] = hn.astype(BF16)


def ffn(h, x, gate, w_in, w_out, layer, norm, last):
    S, D = x.shape
    H = w_out.shape[1]
    tm, th = TILES["ffn"]
    nk = H // th
    row = lambda i, k: (i, 0)
    vec = pl.BlockSpec((1, D), lambda i, k: (0, 0))
    out_shape = [jax.ShapeDtypeStruct((S, D), F32)] + ([] if last else [jax.ShapeDtypeStruct((S, D), BF16)])
    return pl.pallas_call(
        functools.partial(_ffn_kernel, last=last),
        out_shape=out_shape,
        grid=(S // tm, nk),
        in_specs=[pl.BlockSpec((tm, D), row), pl.BlockSpec((tm, D), row), vec,
                  pl.BlockSpec((None, D, th), lambda i, k: (layer, 0, k)),
                  pl.BlockSpec((None, D, th), lambda i, k: (layer, 0, k + nk)),
                  pl.BlockSpec((None, th, D), lambda i, k: (layer, k, 0)),
                  vec, vec, vec],
        out_specs=[pl.BlockSpec((tm, D), row)] * len(out_shape),
        scratch_shapes=[pltpu.VMEM((tm, D), F32)],
        compiler_params=_cparams(("parallel", "arbitrary")),
        name="swiglu_ffn",
    )(h, x, gate, w_in, w_in, w_out, *norm)


def _segsum_heads(x, ones_bd):
    cols = [_dot_ones(x[:, c * LANES:(c + 1) * LANES], ones_bd) for c in range(x.shape[1] // LANES)]
    return jnp.concatenate(cols, axis=1)


def _head_ones():
    r = lax.broadcasted_iota(jnp.int32, (LANES, LANES), 0) // HEAD_DIM
    c = lax.broadcasted_iota(jnp.int32, (LANES, LANES), 1) // HEAD_DIM
    return jnp.where(r == c, 1.0, 0.0).astype(F32)


def _rwkv_prep_kernel(*refs, has_vres, tm):
    (p_ref, pp_ref, mu_ref, w0_ref, a0_ref, kk_w_ref, ka_ref, w2_ref, a2_ref, g2_ref) = refs[:10]
    if has_vres:
        vf_ref, v0_ref, v1_ref, v2_ref = refs[10:14]
        outs = refs[14:]
    else:
        outs = refs[10:]
    r_o, k_o, v_o, kk_o, b_o, cum_o, g_o = outs
    first = pl.program_id(0) == 0
    row = lax.broadcasted_iota(jnp.int32, (tm, 1), 0)

    def mixed(c0, c1):
        p = p_ref[:, c0:c1]
        prev_last = jnp.where(first, 0.0, pp_ref[7:8, c0:c1])
        sh = jnp.where(row == 0, prev_last, pltpu.roll(p, 1, axis=0))
        return p + (sh - p) * mu_ref[:, c0:c1]

    C = RWKV_WIDTH
    r = mixed(0, C)
    k = mixed(C, 2 * C)
    v = mixed(2 * C, 3 * C)
    wa = mixed(3 * C, 3 * C + LANES)
    gl = mixed(3 * C + LANES, RWKV_IN_PAD)

    lora = lambda act, w_ref: _mm3(_split(act), (w_ref[0], w_ref[1]))
    wlog = w0_ref[...] + lora(jnp.tanh(wa), w2_ref)
    xs = -wlog
    softplus = jnp.maximum(xs, 0.0) + jnp.log(1.0 + jnp.exp(-jnp.abs(xs)))
    w = -softplus - 0.5
    log_decay = -jnp.exp(w)
    a = _sigmoid(a0_ref[...] + lora(wa, a2_ref))
    g = lora(_sigmoid(gl), g2_ref)
    if has_vres:
        v = v + (vf_ref[...] - v) * _sigmoid(v0_ref[...] + lora(lora(v, v1_ref), v2_ref))

    kk = k * kk_w_ref[...]
    nrm = jnp.sqrt(_segsum_heads(kk * kk, _head_ones().astype(BF16)))
    kk = kk / jnp.maximum(nrm, 1e-12)
    k = k * (1.0 + (a - 1.0) * ka_ref[...])

    ti = lax.broadcasted_iota(jnp.int32, (tm, tm), 0)
    si = lax.broadcasted_iota(jnp.int32, (tm, tm), 1)
    tri = jnp.where((si <= ti) & (si // CHUNK == ti // CHUNK), 1.0, 0.0).astype(BF16)
    cum = sum(jnp.dot(tri, part, preferred_element_type=F32) for part in _split3(log_decay))

    r_o[...] = r
    k_o[...] = k
    v_o[...] = v
    kk_o[...] = kk
    b_o[...] = kk * a
    cum_o[...] = cum
    g_o[...] = g


def rwkv_prep(p, mu, w0, a0, k_k, k_a, w2p, a2p, g2p, vres):
    S = p.shape[0]
    tm = TILES["rwkv_prep"]
    C = RWKV_WIDTH
    row = lambda i: (i, 0)
    vec = lambda i: (0, 0)
    whole = lambda w: pl.BlockSpec(w.shape, lambda i: (0,) * w.ndim)
    in_specs = [pl.BlockSpec((tm, RWKV_IN_PAD), row),
                pl.BlockSpec((8, RWKV_IN_PAD), lambda i: (jnp.maximum(i * (tm // 8) - 1, 0), 0)),
                pl.BlockSpec((1, RWKV_IN_PAD), vec),
                pl.BlockSpec((1, C), vec), pl.BlockSpec((1, C), vec), pl.BlockSpec((1, C), vec),
                pl.BlockSpec((1, C), vec), whole(w2p), whole(a2p), whole(g2p)]
    args = [p, p, mu, w0, a0, k_k, k_a, w2p, a2p, g2p]
    if vres is not None:
        v_first, v0, v1p, v2p = vres
        in_specs += [pl.BlockSpec((tm, C), row), pl.BlockSpec((1, C), vec), whole(v1p), whole(v2p)]
        args += [v_first, v0, v1p, v2p]
    out = jax.ShapeDtypeStruct((S, C), F32)
    return pl.pallas_call(
        functools.partial(_rwkv_prep_kernel, has_vres=vres is not None, tm=tm),
        out_shape=[out] * 7,
        grid=(S // tm,),
        in_specs=in_specs,
        out_specs=[pl.BlockSpec((tm, C), row)] * 7,
        compiler_params=_cparams(("parallel",)),
        name="rwkv_prep",
    )(*args)


_NN = (((1,), (0,)), ((), ()))
_NT = (((1,), (1,)), ((), ()))
_TN = (((0,), (0,)), ((), ()))


def _split(x):
    hi = x.astype(BF16)
    return hi, (x - hi.astype(F32)).astype(BF16)


def _mm3(a, b, dims=_NN):
    f = lambda x, y: lax.dot_general(x, y, dims, preferred_element_type=F32)
    return f(a[0], b[0]) + (f(a[0], b[1]) + f(a[1], b[0]))


def _split3(x):
    hi = x.astype(BF16)
    r1 = x - hi.astype(F32)
    mid = r1.astype(BF16)
    return hi, mid, (r1 - mid.astype(F32)).astype(BF16)


def _dot_ones(x, ones_bf16):
    hi, mid, lo = _split3(x)
    f = lambda t: jnp.dot(t, ones_bf16, preferred_element_type=F32)
    return f(hi) + (f(mid) + f(lo))


def _rwkv_chunk_kernel(r_ref, k_ref, v_ref, kk_ref, b_ref, cum_ref, g_ref, rk_ref, lnw_ref, lnb_ref,
                       o_ref, ht_ref, y_ref, *, n_chunks, n_pairs):
    L = CHUNK

    @pl.when(pl.program_id(1) == 0)
    def _():
        ht_ref[...] = jnp.zeros_like(ht_ref)

    row = lax.broadcasted_iota(jnp.int32, (L, LANES), 0)
    lane = lax.broadcasted_iota(jnp.int32, (L, LANES), 1)
    head0 = lane < HEAD_DIM
    ri = lax.broadcasted_iota(jnp.int32, (2 * L, 2 * L), 0)
    ci = lax.broadcasted_iota(jnp.int32, (2 * L, 2 * L), 1)
    same = (ri // L) == (ci // L)
    strict = same & ((ci % L) < (ri % L))
    incl = same & ((ci % L) <= (ri % L))

    def stack(x):
        return jnp.concatenate([jnp.where(head0, x, 0.0), jnp.where(head0, 0.0, x)], axis=0)

    items = [(c, p) for c in range(n_chunks) for p in range(n_pairs)]
    pre = [dict() for _ in items]
    for q, (c, p) in zip(pre, items):
        sl = (slice(c * L, (c + 1) * L), slice(p * LANES, (p + 1) * LANES))
        r, k, v, kk, b, cum = r_ref[sl], k_ref[sl], v_ref[sl], kk_ref[sl], b_ref[sl], cum_ref[sl]
        cum_excl = jnp.where(row == 0, 0.0, pltpu.roll(cum, 1, axis=0))
        e_pos = jnp.exp(cum)
        e_neg = jnp.exp(-cum)
        q["s_a"] = stack(-kk * jnp.exp(cum_excl))
        q["v"] = stack(v)
        q["s_r"] = stack(r * e_pos)
        sp_ar = _split(jnp.concatenate([q["s_a"], q["s_r"]], axis=0))
        q["bk"] = _split(jnp.concatenate([stack(b * e_neg), stack(k * e_neg)], axis=0))
        q["p_last"] = e_pos[L - 1:L, :]
        q["big"] = _mm3(sp_ar, q["bk"], _NT)
    for q in pre:
        big = q.pop("big")
        q["pw"] = jnp.where(strict, big[:2 * L, :2 * L], 0.0)
        q["a_ak"] = jnp.where(strict, big[:2 * L, 2 * L:], 0.0)
        q["rbk"] = _split(jnp.concatenate([jnp.where(incl, big[2 * L:, :2 * L], 0.0),
                                           jnp.where(incl, big[2 * L:, 2 * L:], 0.0)], axis=1))
    mm = lambda x, y: _mm3(_split(x), _split(y))
    mmb = lambda x, y: jnp.dot(x.astype(BF16), y.astype(BF16), preferred_element_type=F32)
    m0 = 8
    blk_of = lambda m: ((ri // m) == (ci // m))
    for q in pre:
        d1 = jnp.where(blk_of(m0), q["pw"], 0.0)
        q["d1"] = d1
        q["d2"] = mmb(d1, d1)
    for q in pre:
        d1, d2 = q.pop("d1"), q["d2"]
        q["n"] = d1 + d2 + mmb(d1, d2)
        q["d4"] = mmb(d2, d2)
        q.pop("d2")
    for q in pre:
        d4 = q.pop("d4")
        q["n"] = q["n"] + d4 + mmb(q["n"], d4)
    m = m0
    while m < L:
        for q in pre:
            e = jnp.where(blk_of(2 * m) & jnp.logical_not(blk_of(m)), q["pw"], 0.0)
            q["x"] = e + mmb(q["n"], e)
        for q in pre:
            x = q.pop("x")
            q["n"] = q["n"] + x + mmb(x, q["n"])
        m *= 2
    for q in pre:
        q.pop("pw")
        q["tmp"] = mm(q.pop("a_ak"), q["v"])
    for q in pre:
        x = jnp.concatenate([q.pop("s_a"), q.pop("tmp")], axis=1)
        q["wu"] = x + mm(q.pop("n"), x)
    for q in pre:
        wu = q.pop("wu")
        sp_w = _split(wu[:, :2 * L])
        sp_uv0 = _split(jnp.concatenate([wu[:, 2 * L:], q.pop("v")], axis=0))
        bk, rbk = q.pop("bk"), q.pop("rbk")
        q["mt"] = _split(_mm3(sp_w, (bk[0][:2 * L], bk[1][:2 * L]), _TN))
        q["gt"] = _mm3(sp_uv0, bk, _TN)
        q["rhat"] = _split(q.pop("s_r") + _mm3((rbk[0][:, :2 * L], rbk[1][:, :2 * L]), sp_w))
        q["yhat"] = _mm3(rbk, sp_uv0)

    hts = [ht_ref[p] for p in range(n_pairs)]
    for q, (c, p) in zip(pre, items):
        sp_h = _split(hts[p])
        y_s = _mm3(q["rhat"], sp_h, _NT) + q["yhat"]
        hts[p] = (hts[p] + _mm3(sp_h, q["mt"]) + q["gt"]) * q["p_last"]
        y_ref[c * L:(c + 1) * L, p * LANES:(p + 1) * LANES] = y_s[:L] + y_s[L:]
    for p in range(n_pairs):
        ht_ref[p] = hts[p]

    ones_bd = _head_ones().astype(BF16)
    inv_n = 1.0 / HEAD_DIM
    for p in range(n_pairs):
        cs = slice(p * LANES, (p + 1) * LANES)
        y = y_ref[:, cs]
        mean = _dot_ones(y, ones_bd) * inv_n
        yc = y - mean
        var = _dot_ones(yc * yc, ones_bd) * inv_n
        yn = yc * lax.rsqrt(var + GN_EPS) * lnw_ref[:, cs] + lnb_ref[:, cs]
        r, k, v = r_ref[:, cs], k_ref[:, cs], v_ref[:, cs]
        bonus = _dot_ones(r * k * rk_ref[:, cs], ones_bd)
        o_ref[:, cs] = ((yn + bonus * v) * g_ref[:, cs]).astype(o_ref.dtype)


def rwkv_chunk(r, k, v, kk, b, cum, g, r_k, ln_w, ln_b):
    S, C = r.shape
    n_chunks, n_pairs = TILES["rwkv_chunk"]
    tm = n_chunks * CHUNK
    tw = n_pairs * LANES
    blk = pl.BlockSpec((tm, tw), lambda h, c: (c, h))
    vec = pl.BlockSpec((1, tw), lambda h, c: (0, h))
    return pl.pallas_call(
        functools.partial(_rwkv_chunk_kernel, n_chunks=n_chunks, n_pairs=n_pairs),
        out_shape=jax.ShapeDtypeStruct((S, C), BF16),
        grid=(C // tw, S // tm),
        in_specs=[blk] * 7 + [vec] * 3,
        out_specs=blk,
        scratch_shapes=[pltpu.VMEM((n_pairs, LANES, LANES), F32), pltpu.VMEM((tm, tw), F32)],
        compiler_params=_cparams(("parallel", "arbitrary")),
        name="rwkv_chunk",
    )(r, k, v, kk, b, cum, g, r_k, ln_w, ln_b)


def _gelu_tanh(y):
    return 0.5 * y * (1.0 + jnp.tanh(math.sqrt(2.0 / math.pi) * (y + 0.044715 * (y * y * y))))


def _cmul(ar, ai, br, bi):
    return ar * br - ai * bi, ar * bi + ai * br


def _s5_kernel(u_ref, kall_ref, wz_ref, call_ref, pw_ref, ct_ref, d_ref, o_ref, carry_ref, ys_ref, *, tm, sub):
    half = 8 * S5_STATE
    Q = S5_LAG
    nq = tm // Q

    @pl.when(pl.program_id(1) == 0)
    def _():
        carry_ref[...] = jnp.zeros_like(carry_ref)

    zin = jnp.concatenate([u_ref[pl.ds(Q - 1 - tau, nq, stride=Q), :] for tau in range(Q)], axis=1)
    z = jnp.dot(zin.astype(BF16), wz_ref[0], preferred_element_type=F32)

    xr, xi = z[:, :half], z[:, half:]
    crow = lax.broadcasted_iota(jnp.int32, (nq, 1), 0)
    for lvl in range(int(math.log2(nq))):
        off = 1 << lvl
        ar, ai = pw_ref[0, lvl:lvl + 1, :half], pw_ref[0, lvl:lvl + 1, half:]
        keep = crow >= off
        sr = jnp.where(keep, pltpu.roll(xr, off, axis=0), 0.0)
        si = jnp.where(keep, pltpu.roll(xi, off, axis=0), 0.0)
        xr, xi = xr + ar * sr - ai * si, xi + ar * si + ai * sr
    cr, ci = carry_ref[:, :half], carry_ref[:, half:]
    tr, ti = _cmul(ct_ref[0, :, :half], ct_ref[0, :, half:], cr, ci)
    xr, xi = xr + tr, xi + ti
    er = jnp.where(crow == 0, cr, pltpu.roll(xr, 1, axis=0))
    ei = jnp.where(crow == 0, ci, pltpu.roll(xi, 1, axis=0))
    carry_ref[:, :half] = xr[nq - 1:nq, :]
    carry_ref[:, half:] = xi[nq - 1:nq, :]

    ysf = jnp.dot(jnp.concatenate([er, ei], axis=1).astype(BF16), call_ref[0],
                  preferred_element_type=F32)
    for s in range(Q):
        ys_ref[pl.ds(s, nq, stride=Q), :] = ysf[:, s * LANES:(s + 1) * LANES]

    pos = lax.broadcasted_iota(jnp.int32, (sub, 1), 0) % Q
    for t0 in range(0, tm, sub):
        u = u_ref[t0:t0 + sub, :]
        lags = [u] + [jnp.where(pos >= tau, pltpu.roll(u, tau, axis=0), 0.0) for tau in range(1, Q)]
        y = jnp.dot(jnp.concatenate(lags, axis=1).astype(BF16), kall_ref[0], preferred_element_type=F32)
        o_ref[t0:t0 + sub, :] = _gelu_tanh(y + ys_ref[t0:t0 + sub, :] + d_ref[...] * u)


def s5_scan(p, col0, tables, d):
    kall, wz, call, pows, ctab = tables
    tm, sub = TILES["s5"]
    S = p.shape[0]
    nj = S5_WIDTH // LANES
    nst = 2 * 8 * S5_STATE
    nq = tm // S5_LAG
    ql = S5_LAG * LANES
    per_j = lambda *shape: pl.BlockSpec((1,) + shape, lambda j, i: (j,) + (0,) * len(shape))
    return pl.pallas_call(
        functools.partial(_s5_kernel, tm=tm, sub=sub),
        out_shape=jax.ShapeDtypeStruct((S, S5_WIDTH), F32),
        grid=(nj, S // tm),
        in_specs=[pl.BlockSpec((tm, LANES), lambda j, i: (i, col0 // LANES + j)),
                  per_j(ql, LANES), per_j(ql, nst), per_j(nst, ql),
                  per_j(pows.shape[1], nst), per_j(nq, nst),
                  pl.BlockSpec((1, LANES), lambda j, i: (0, j))],
        out_specs=pl.BlockSpec((tm, LANES), lambda j, i: (i, j)),
        scratch_shapes=[pltpu.VMEM((1, nst), F32), pltpu.VMEM((tm, LANES), F32)],
        compiler_params=_cparams(("parallel", "arbitrary")),
        name="s5_scan",
    )(p, kall, wz, call, pows, ctab, d)


def _s5_tables(lam_re, lam_im, log_step, b_re, b_im, c_re, c_im):
    G, P, GS = S5_WIDTH // S5_GROUP, S5_STATE, S5_GROUP
    nj = G // 8
    Q = S5_LAG
    nq = TILES["s5"][0] // Q
    lam = lax.complex(jnp.minimum(lam_re, -1e-4), lam_im)
    step = jnp.exp(log_step)[:, None]
    lam_dt = lam * step
    lam_bar = jnp.exp(lam_dt)
    b_bar = ((lam_bar - 1.0) / lam)[..., None] * lax.complex(b_re, b_im)

    def lam_pow(e):
        return jnp.exp(lam_dt[None] * e.astype(F32)[:, None, None])

    def pack(z):
        z = jnp.moveaxis(z.reshape(z.shape[0], nj, 8 * P), 1, 0)
        return jnp.concatenate([jnp.real(z), jnp.imag(z)], axis=-1).astype(F32)

    pows = pack(lam_pow(Q * 2 ** jnp.arange(int(math.log2(nq)))))
    ctab = pack(lam_pow(Q * (jnp.arange(nq) + 1)))

    eye8 = jnp.eye(8, dtype=F32)
    ql = Q * 8 * GS
    grp_of_ql = (jnp.arange(ql) // GS) % 8
    grp_of_state = jnp.arange(8 * P) // P
    m = lam_pow(jnp.arange(Q))[..., None] * b_bar[None]
    def bd_in(x):
        rows = jnp.transpose(x.reshape(Q, nj, 8, P, GS), (1, 0, 2, 4, 3)).reshape(nj, ql, P)
        return jnp.where(grp_of_ql[:, None] == grp_of_state[None, :], jnp.tile(rows, (1, 1, 8)), 0.0)
    wz = jnp.concatenate([bd_in(jnp.real(m)), bd_in(jnp.imag(m))], axis=-1).astype(F32)
    ls = lam_pow(jnp.arange(Q) + 1)[:, :, None, :]
    lr, li = jnp.real(ls), jnp.imag(ls)
    def bd_out(x):
        cols = jnp.transpose(x.reshape(Q, nj, 8, GS, P), (1, 4, 0, 2, 3)).reshape(nj, P, ql)
        return jnp.where(grp_of_state[:, None] == grp_of_ql[None, :], jnp.tile(cols, (1, 8, 1)), 0.0)
    call = jnp.concatenate([bd_out(c_re[None] * lr - c_im[None] * li),
                            -bd_out(c_re[None] * li + c_im[None] * lr)], axis=1).astype(F32)
    ktab = (jnp.einsum('ghp,tgpk->tghk', c_re, jnp.real(m), precision=HI)
            - jnp.einsum('ghp,tgpk->tghk', c_im, jnp.imag(m), precision=HI))
    kall = jnp.einsum('tjghk,gm->jtgkmh', ktab.reshape(Q, nj, 8, GS, GS), eye8).reshape(nj, Q * 8 * GS, 8 * GS)
    return kall.astype(BF16), wz.astype(BF16), call.astype(BF16), pows, ctab


def _glu_kernel(y_ref, yc_ref, w_ref, b_ref, o_ref, a_ref):
    @pl.when(pl.program_id(1) == 0)
    def _():
        a_ref[...] = y_ref[...].astype(BF16)

    z = _dot(a_ref[...], w_ref[...]) + b_ref[...]
    o_ref[...] = (yc_ref[...] * _sigmoid(z)).astype(o_ref.dtype)


def glu(y, w, layer, b):
    S, C = y.shape
    tm, tn = TILES["glu"]
    return pl.pallas_call(
        _glu_kernel,
        out_shape=jax.ShapeDtypeStruct((S, C), BF16),
        grid=(S // tm, C // tn),
        in_specs=[pl.BlockSpec((tm, C), lambda i, j: (i, 0)), pl.BlockSpec((tm, tn), lambda i, j: (i, j)),
                  pl.BlockSpec((None, C, tn), lambda i, j: (layer, 0, j)),
                  pl.BlockSpec((1, tn), lambda i, j: (0, j))],
        out_specs=pl.BlockSpec((tm, tn), lambda i, j: (i, j)),
        scratch_shapes=[pltpu.VMEM((tm, C), BF16)],
        compiler_params=_cparams(("parallel", "arbitrary")),
        name="s5_glu",
    )(y, y, w, b)


def _attn_block(qs, ks, vs, bias, head0):
    zero = jnp.zeros((), qs.dtype)
    pvs, dens, lses = [], [], []
    for hmask in (head0, jnp.logical_not(head0)):
        s = _dot_nt(jnp.where(hmask, qs, zero), ks) + bias
        m = jnp.max(s, axis=-1, keepdims=True)
        pexp = jnp.exp(s - m)
        den = jnp.sum(pexp, axis=-1, keepdims=True)
        pvs.append(_dot(pexp.astype(vs.dtype), vs))
        dens.append(den)
        lses.append(m + jnp.log(den))
    out = jnp.where(head0, pvs[0], pvs[1]) / jnp.where(head0, dens[0], dens[1])
    return out, jnp.where(head0, lses[0], lses[1])


def _attn_kernel(*refs, dilations, tile):
    nb = len(dilations)
    in_refs = [refs[5 * b:5 * b + 5] for b in range(nb)]
    out_ref, o_s, l_s = refs[5 * nb:]
    B = ATTN_BLK
    first_tile = pl.program_id(1) == 0
    head0 = lax.broadcasted_iota(jnp.int32, (B, LANES), 1) < HEAD_DIM
    qi = lax.broadcasted_iota(jnp.int32, (B, 2 * B), 0)
    kj = lax.broadcasted_iota(jnp.int32, (B, 2 * B), 1)
    in_band = (kj >= qi) & (kj <= qi + B)
    band = jnp.where(in_band, 0.0, -jnp.inf)
    band_first = jnp.where(in_band & (kj >= jnp.where(first_tile, B, 0)), 0.0, -jnp.inf)
    scale = HEAD_DIM ** -0.5

    for b, d in enumerate(dilations):
        q_ref, kc_ref, kp_ref, vc_ref, vp_ref = in_refs[b]
        n_blk = tile // d // B

        def block(r, row0, first_blk, b=b, d=d, refs_=in_refs[b]):
            q_ref, kc_ref, kp_ref, vc_ref, vp_ref = refs_
            qs = q_ref[r, pl.ds(row0, B), :] * scale
            if first_blk:
                ks = jnp.concatenate([kp_ref[r], kc_ref[r, :B, :]], axis=0)
                vs = jnp.concatenate([vp_ref[r], vc_ref[r, :B, :]], axis=0)
            else:
                ks = kc_ref[r, pl.ds(row0 - B, 2 * B), :]
                vs = vc_ref[r, pl.ds(row0 - B, 2 * B), :]
            o, lse = _attn_block(qs, ks, vs, band_first if first_blk else band, head0)
            dst = pl.ds(row0 * d + r, B, stride=d) if d > 1 else pl.ds(row0, B)
            o_s[b, dst, :] = o
            l_s[b, dst, :] = lse

        for r in range(d):
            for qb in range(n_blk):
                block(r, qb * B, qb == 0)

    rc = 2 * B

    def merge(c, carry):
        sl = pl.ds(pl.multiple_of(c * rc, rc), rc)
        ls_ = [l_s[b, sl, :] for b in range(nb)]
        m = functools.reduce(jnp.maximum, ls_)
        es = [jnp.exp(l - m) for l in ls_]
        num = functools.reduce(lambda x, y: x + y, [e * o_s[b, sl, :] for b, e in enumerate(es)])
        den = functools.reduce(lambda x, y: x + y, es)
        out_ref[sl, :] = (num / den).astype(out_ref.dtype)
        return carry
    lax.fori_loop(0, tile // rc, merge, 0)


def dilated_attention(qkvs, tile=ATTN_TILE):
    dilations = tuple(t.shape[0] for t in qkvs)
    S = qkvs[0].shape[0] * qkvs[0].shape[1]
    nh = D_MODEL // LANES
    in_specs, args = [], []
    for t, d in zip(qkvs, dilations):
        rows = tile // d
        rpb = rows // ATTN_BLK

        def cur(part, rows=rows, d=d):
            return pl.BlockSpec((d, rows, LANES), lambda h, i: (0, i, part * nh + h))

        def prev(part, rpb=rpb, d=d):
            return pl.BlockSpec((d, ATTN_BLK, LANES),
                                lambda h, i: (0, jnp.maximum(i * rpb - 1, 0), part * nh + h))

        in_specs += [cur(0), cur(1), prev(1), cur(2), prev(2)]
        args += [t] * 5
    nb = len(dilations)
    return pl.pallas_call(
        functools.partial(_attn_kernel, dilations=dilations, tile=tile),
        out_shape=jax.ShapeDtypeStruct((S, D_MODEL), BF16),
        grid=(nh, S // tile),
        in_specs=in_specs,
        out_specs=pl.BlockSpec((tile, LANES), lambda h, i: (i, h)),
        scratch_shapes=[pltpu.VMEM((nb, tile, LANES), F32), pltpu.VMEM((nb, tile, LANES), F32)],
        compiler_params=_cparams(("parallel", "arbitrary")),
        name="dilated_attention",
    )(*args)


def _rotary_tables(S):
    half = ROT_DIM // 2
    inv = ROPE_THETA ** (-jnp.arange(half, dtype=F32) * 2.0 / ROT_DIM)
    ang = jnp.arange(S, dtype=F32)[:, None] * inv[None, :]
    cos, sin = jnp.cos(ang), jnp.sin(ang)
    rest = HEAD_DIM - ROT_DIM
    ones = jnp.ones((S, rest), F32)
    zeros = jnp.zeros((S, rest), F32)
    zh = jnp.zeros((S, half), F32)
    cos_t = jnp.concatenate([cos, cos, ones], axis=1)
    sin_a = jnp.concatenate([-sin, zh, zeros], axis=1)
    sin_b = jnp.concatenate([zh, sin, zeros], axis=1)
    rep = LANES // HEAD_DIM
    return tuple(jnp.tile(t, (1, rep)) for t in (cos_t, sin_a, sin_b))


def _pad_rows(w, n):
    return jnp.pad(w, ((0, n - w.shape[0]), (0, 0)))


def _split_w(w):
    hi = w.astype(BF16)
    return jnp.stack([hi, (w - hi.astype(F32)).astype(BF16)])


def kernel(x, c, ada_w, ada_b, norm_mix_g, norm_ffn_g, hyb_w_in, hyb_w_out, rwkv_mu, rwkv_w0, rwkv_w2, rwkv_a0, rwkv_a2, rwkv_g2, rwkv_k_k, rwkv_k_a, rwkv_r_k, rwkv_ln_w, rwkv_ln_b, rwkv_v0, rwkv_v1, rwkv_v2, s5_lam_re, s5_lam_im, s5_log_step, s5_b_re, s5_b_im, s5_c_re, s5_c_im, s5_d, s5_glu_w, s5_glu_b, attn_w_qkv, attn_w_o, ffn_w_in, ffn_w_out, final_norm_g):
    B, S, D = x.shape
    assert B == 1 and D == D_MODEL and S % 2048 == 0
    C = RWKV_WIDTH
    xs = x.reshape(S, D)
    mod = ada_modulation(c, ada_w, ada_b)
    rot_tables = _rotary_tables(S)
    dilations = tuple(d for (_, d) in DILATED_BRANCHES)
    row = lambda t: t.reshape(1, -1)
    mods = [[mod[i][:, q * D:(q + 1) * D] for q in range(6)] for i in range(DEPTH)]
    mix_norm = lambda i: (row(norm_mix_g[i]), mods[i][0], mods[i][1])
    ffn_norm = lambda i: (row(norm_ffn_g[i]), mods[i][3], mods[i][4])

    gpad = RWKV_IN_PAD - RWKV_IN
    n_even = hyb_w_in.shape[0]
    w_hyb_in = jnp.concatenate([hyb_w_in[:, :, :RWKV_IN], jnp.zeros((n_even, D, gpad), F32), hyb_w_in[:, :, RWKV_IN:],
                                jnp.zeros((n_even, D, EVEN_IN_TILED - EVEN_IN_PAD), F32)], axis=2).astype(BF16)
    w_hyb_out, w_glu = hyb_w_out.astype(BF16), s5_glu_w.astype(BF16)
    w_qkv, w_attn_o = attn_w_qkv.astype(BF16), attn_w_o.astype(BF16)
    w_ffn_in, w_ffn_out = ffn_w_in.astype(BF16), ffn_w_out.astype(BF16)

    v_first = None
    h = norm_mod_rows(xs, mix_norm(0))
    for i in range(DEPTH):
        gt_mix, gt_ffn = mods[i][2], mods[i][5]
        j = i // 2
        if i % 2 == 0:
            mu = jnp.concatenate([rwkv_mu[j], jnp.zeros((gpad,), F32)]).reshape(1, RWKV_IN_PAD)
            p = in_proj(h, w_hyb_in, j)

            w2p = _split_w(jnp.concatenate([rwkv_w2[j], jnp.zeros((LORA_A, C), F32)], axis=0))
            a2p = _split_w(jnp.concatenate([jnp.zeros((LORA_W, C), F32), rwkv_a2[j]], axis=0))
            g2p = _split_w(_pad_rows(rwkv_g2[j], 2 * LANES))
            vres = None
            if j > 0:
                v1p = _split_w(jnp.pad(rwkv_v1[j - 1], ((0, 0), (0, LANES - LORA_V))))
                v2p = _split_w(_pad_rows(rwkv_v2[j - 1], LANES))
                vres = (v_first, row(rwkv_v0[j - 1]), v1p, v2p)
            r, k, v, kk, b, cum, g = rwkv_prep(p, mu, row(rwkv_w0[j]), row(rwkv_a0[j]), row(rwkv_k_k[j]),
                                               row(rwkv_k_a[j]), w2p, a2p, g2p, vres)
            if j == 0:
                v_first = v
            y_rwkv = rwkv_chunk(r, k, v, kk, b, cum, g, row(rwkv_r_k[j]), row(rwkv_ln_w[j]),
                                row(rwkv_ln_b[j]))

            s5_tabs = _s5_tables(s5_lam_re[j], s5_lam_im[j], s5_log_step[j], s5_b_re[j], s5_b_im[j],
                                 s5_c_re[j], s5_c_im[j])
            y_s5 = s5_scan(p, RWKV_IN_PAD, s5_tabs, row(s5_d[j]))
            y_s5 = glu(y_s5, w_glu, j, row(s5_glu_b[j]))
            xs, h = proj_residual([y_rwkv, y_s5], w_hyb_out, j, xs, gt_mix, ffn_norm(i))
        else:
            qkvs = qkv_proj(h, w_qkv, j, rot_tables, 2 * D, dilations)
            o = dilated_attention(qkvs)
            xs, h = proj_residual([o], w_attn_o, j, xs, gt_mix, ffn_norm(i))
        if i < DEPTH - 1:
            xs, h = ffn(h, xs, gt_ffn, w_ffn_in, w_ffn_out, i, mix_norm(i + 1), last=False)
        else:
            zero = jnp.zeros((1, D), F32)
            (xs,) = ffn(h, xs, gt_ffn, w_ffn_in, w_ffn_out, i, (row(final_norm_g), zero, zero), last=True)
    return xs.reshape(B, S, D)
```

```python
import functools
import math

import jax
import jax.numpy as jnp
from jax import lax
from jax.experimental import pallas as pl
from jax.experimental.pallas import tpu as pltpu

F32 = jnp.float32
BF16 = jnp.bfloat16
HI = lax.Precision.HIGHEST

D_MODEL = 2048
DEPTH = 4
HEAD_DIM = 64
RWKV_WIDTH = 1024
LORA_W = 64
LORA_A = 64
LORA_V = 32
LORA_G = 160
RWKV_IN = 3 * RWKV_WIDTH + LORA_W + LORA_A + LORA_G
RWKV_IN_PAD = 3456
S5_WIDTH = 1024
S5_GROUP = 16
S5_STATE = 64
S5_LAG = 8
EVEN_IN_PAD = RWKV_IN_PAD + S5_WIDTH
EVEN_IN_TILED = 4608
ATTN_HEADS = 32
ROT_DIM = 16
ROPE_THETA = 500000.0
DILATED_BRANCHES = ((128, 1), (512, 4), (2048, 16))
ATTN_BLK = 128
ATTN_TILE = 2048
FFN_HIDDEN = 5632
RMS_EPS = 1e-6
GN_EPS = 64e-5

LANES = 128
CHUNK = 64
VMEM_LIMIT = 48 * 1024 * 1024

TILES = {
    "ada": 1024,
    "norm_rows": 512,
    "in_proj": (1024, 768),
    "qkv_proj": (2048, 512),
    "out_proj": 512,
    "ffn": (512, 512),
    "glu": (512, 512),
    "rwkv_prep": 256,
    "rwkv_chunk": (4, 4),
    "s5": (2048, 256),
}


def _cparams(sem):
    return pltpu.CompilerParams(dimension_semantics=sem, vmem_limit_bytes=VMEM_LIMIT)


def _sigmoid(x):
    return 1.0 / (1.0 + jnp.exp(-x))


def _dot(a, b):
    return jnp.dot(a, b, preferred_element_type=F32)


def _dot_nt(a, b):
    return lax.dot_general(a, b, (((1,), (1,)), ((), ())), preferred_element_type=F32)


def _norm_mod(x, g, shift, scale):
    ms = jnp.mean(x * x, axis=-1, keepdims=True)
    return (x * lax.rsqrt(ms + RMS_EPS) * g) * (1.0 + scale) + shift


def _ada_kernel(c_ref, w_ref, b_ref, o_ref):
    c = c_ref[...]
    s = c * _sigmoid(c)
    o_ref[0] = jnp.sum(s * w_ref[0], axis=0, keepdims=True) + b_ref[0]


def ada_modulation(c, ada_w, ada_b):
    L, D, N = ada_w.shape
    tn = TILES["ada"]
    return pl.pallas_call(
        _ada_kernel,
        out_shape=jax.ShapeDtypeStruct((L, 1, N), F32),
        grid=(L, N // tn),
        in_specs=[pl.BlockSpec((D, 1), lambda l, j: (0, 0)),
                  pl.BlockSpec((1, D, tn), lambda l, j: (l, 0, j)),
                  pl.BlockSpec((1, 1, tn), lambda l, j: (l, 0, j))],
        out_specs=pl.BlockSpec((1, 1, tn), lambda l, j: (l, 0, j)),
        compiler_params=_cparams(("parallel", "parallel")),
        name="ada_modulation",
    )(c.reshape(D, 1), ada_w, ada_b.reshape(L, 1, N))


def _norm_mod_kernel(x_ref, g_ref, sh_ref, sc_ref, h_ref):
    h_ref[...] = _norm_mod(x_ref[...], g_ref[...], sh_ref[...], sc_ref[...]).astype(BF16)


def norm_mod_rows(x, norm):
    S, D = x.shape
    tm = TILES["norm_rows"]
    vec = pl.BlockSpec((1, D), lambda i: (0, 0))
    return pl.pallas_call(
        _norm_mod_kernel,
        out_shape=jax.ShapeDtypeStruct((S, D), BF16),
        grid=(S // tm,),
        in_specs=[pl.BlockSpec((tm, D), lambda i: (i, 0)), vec, vec, vec],
        out_specs=pl.BlockSpec((tm, D), lambda i: (i, 0)),
        compiler_params=_cparams(("parallel",)),
        name="norm_mod_rows",
    )(x, *norm)


def _matmul_kernel(h_ref, w_ref, o_ref):
    o_ref[...] = _dot(h_ref[...], w_ref[...]).astype(o_ref.dtype)


def in_proj(h, w, layer):
    S, D = h.shape
    N = w.shape[2]
    tm, tn = TILES["in_proj"]
    return pl.pallas_call(
        _matmul_kernel,
        out_shape=jax.ShapeDtypeStruct((S, N), F32),
        grid=(S // tm, N // tn),
        in_specs=[pl.BlockSpec((tm, D), lambda i, j: (i, 0)),
                  pl.BlockSpec((None, D, tn), lambda i, j: (layer, 0, j))],
        out_specs=pl.BlockSpec((tm, tn), lambda i, j: (i, j)),
        compiler_params=_cparams(("parallel", "parallel")),
        name="in_proj",
    )(h, w)


def _qkv_kernel(h_ref, w_ref, cos_ref, sa_ref, sb_ref, *rest, n_rot_blocks, dilations):
    o_refs, t_ref = rest[:len(dilations)], rest[len(dilations)]
    j = pl.program_id(1)
    n_lane_blocks, tm, _ = t_ref.shape
    acc = _dot(h_ref[...], w_ref[...])

    @pl.when(j < n_rot_blocks)
    def _():
        cos, sa, sb = cos_ref[...], sa_ref[...], sb_ref[...]
        half = ROT_DIM // 2
        for c in range(n_lane_blocks):
            t = acc[:, c * LANES:(c + 1) * LANES]
            t_ref[c] = t * cos + pltpu.roll(t, LANES - half, axis=1) * sa + pltpu.roll(t, half, axis=1) * sb

    @pl.when(j >= n_rot_blocks)
    def _():
        for c in range(n_lane_blocks):
            t_ref[c] = acc[:, c * LANES:(c + 1) * LANES]

    for d, o_ref in zip(dilations, o_refs):
        for c in range(n_lane_blocks):
            cs = slice(c * LANES, (c + 1) * LANES)
            if d == 1:
                o_ref[0, :, cs] = t_ref[c].astype(o_ref.dtype)
                continue
            for r in range(d):
                o_ref[r, :, cs] = t_ref[c, pl.ds(r, tm // d, stride=d), :].astype(o_ref.dtype)


def qkv_proj(h, w, layer, rot_tables, n_rot_cols, dilations):
    S, D = h.shape
    N = w.shape[2]
    tm, tn = TILES["qkv_proj"]
    return pl.pallas_call(
        functools.partial(_qkv_kernel, n_rot_blocks=n_rot_cols // tn, dilations=dilations),
        out_shape=[jax.ShapeDtypeStruct((d, S // d, N), BF16) for d in dilations],
        grid=(S // tm, N // tn),
        in_specs=[pl.BlockSpec((tm, D), lambda i, j: (i, 0)),
                  pl.BlockSpec((None, D, tn), lambda i, j: (layer, 0, j))]
                 + [pl.BlockSpec((tm, LANES), lambda i, j: (i, 0))] * 3,
        out_specs=[pl.BlockSpec((d, tm // d, tn), lambda i, j: (0, i, j)) for d in dilations],
        scratch_shapes=[pltpu.VMEM((tn // LANES, tm, LANES), F32)],
        compiler_params=_cparams(("parallel", "parallel")),
        name="qkv_proj",
    )(h, w, *rot_tables)


def _proj_res_kernel(*refs, n_a):
    a_refs, w_refs = refs[:n_a], refs[n_a:2 * n_a]
    x_ref, gt_ref, g_ref, sh_ref, sc_ref, o_ref, h_ref = refs[2 * n_a:]
    acc = _dot(a_refs[0][...], w_refs[0][...])
    for a_ref, w_ref in zip(a_refs[1:], w_refs[1:]):
        acc = acc + _dot(a_ref[...], w_ref[...])
    xn = x_ref[...] + gt_ref[...] * acc
    o_ref[...] = xn
    h_ref[...] = _norm_mod(xn, g_ref[...], sh_ref[...], sc_ref[...]).astype(BF16)


def proj_residual(a_list, w, layer, x, gate, next_norm):
    S, N = x.shape
    tm = TILES["out_proj"]
    n_a = len(a_list)
    row = lambda i: (i, 0)
    vec = pl.BlockSpec((1, N), lambda i: (0, 0))
    in_specs, args = [], []
    for a in a_list:
        in_specs.append(pl.BlockSpec((tm, a.shape[1]), row))
        args.append(a)
    off = 0
    for a in a_list:
        ka = a.shape[1]
        in_specs.append(pl.BlockSpec((None, ka, N), functools.partial(lambda i, b: (layer, b, 0), b=off // ka)))
        args.append(w)
        off += ka
    in_specs += [pl.BlockSpec((tm, N), row), vec, vec, vec, vec]
    args += [x, gate, *next_norm]
    return pl.pallas_call(
        functools.partial(_proj_res_kernel, n_a=n_a),
        out_shape=[jax.ShapeDtypeStruct((S, N), F32), jax.ShapeDtypeStruct((S, N), BF16)],
        grid=(S // tm,),
        in_specs=in_specs,
        out_specs=[pl.BlockSpec((tm, N), row), pl.BlockSpec((tm, N), row)],
        compiler_params=_cparams(("parallel",)),
        name="proj_residual",
    )(*args)


def _ffn_kernel(h_ref, x_ref, gt_ref, wg_ref, wu_ref, wo_ref, g_ref, sh_ref, sc_ref, *rest, last):
    if last:
        o_ref, acc_ref = rest
    else:
        o_ref, hn_ref, acc_ref = rest
    k = pl.program_id(1)

    @pl.when(k == 0)
    def _():
        acc_ref[...] = jnp.zeros_like(acc_ref)

    h = h_ref[...]
    gate = _dot(h, wg_ref[...])
    up = _dot(h, wu_ref[...])
    act = (gate * _sigmoid(gate) * up).astype(BF16)
    acc_ref[...] += _dot(act, wo_ref[...])

    @pl.when(k == pl.num_programs(1) - 1)
    def _():
        xn = x_ref[...] + gt_ref[...] * acc_ref[...]
        hn = _norm_mod(xn, g_ref[...], sh_ref[...], sc_ref[...])
        if last:
            o_ref[...] = hn
        else:
            o_ref[...] = xn
            hn_ref[...] = hn.astype(BF16)


def ffn(h, x, gate, w_in, w_out, layer, norm, last):
    S, D = x.shape
    H = w_out.shape[1]
    tm, th = TILES["ffn"]
    nk = H // th
    row = lambda i, k: (i, 0)
    vec = pl.BlockSpec((1, D), lambda i, k: (0, 0))
    out_shape = [jax.ShapeDtypeStruct((S, D), F32)] + ([] if last else [jax.ShapeDtypeStruct((S, D), BF16)])
    return pl.pallas_call(
        functools.partial(_ffn_kernel, last=last),
        out_shape=out_shape,
        grid=(S // tm, nk),
        in_specs=[pl.BlockSpec((tm, D), row), pl.BlockSpec((tm, D), row), vec,
                  pl.BlockSpec((None, D, th), lambda i, k: (layer, 0, k)),
                  pl.BlockSpec((None, D, th), lambda i, k: (layer, 0, k + nk)),
                  pl.BlockSpec((None, th, D), lambda i, k: (layer, k, 0)),
                  vec, vec, vec],
        out_specs=[pl.BlockSpec((tm, D), row)] * len(out_shape),
        scratch_shapes=[pltpu.VMEM((tm, D), F32)],
        compiler_params=_cparams(("parallel", "arbitrary")),
        name="swiglu_ffn",
    )(h, x, gate, w_in, w_in, w_out, *norm)


def _segsum_heads(x, ones_bd):
    cols = [_dot_ones(x[:, c * LANES:(c + 1) * LANES], ones_bd) for c in range(x.shape[1] // LANES)]
    return jnp.concatenate(cols, axis=1)


def _head_ones():
    r = lax.broadcasted_iota(jnp.int32, (LANES, LANES), 0) // HEAD_DIM
    c = lax.broadcasted_iota(jnp.int32, (LANES, LANES), 1) // HEAD_DIM
    return jnp.where(r == c, 1.0, 0.0).astype(F32)


def _rwkv_prep_kernel(*refs, has_vres, tm):
    (p_ref, pp_ref, mu_ref, w0_ref, a0_ref, kk_w_ref, ka_ref, w2_ref, a2_ref, g2_ref) = refs[:10]
    if has_vres:
        vf_ref, v0_ref, v1_ref, v2_ref = refs[10:14]
        outs = refs[14:]
    else:
        outs = refs[10:]
    r_o, k_o, v_o, kk_o, b_o, cum_o, g_o = outs
    first = pl.program_id(0) == 0
    row = lax.broadcasted_iota(jnp.int32, (tm, 1), 0)

    def mixed(c0, c1):
        p = p_ref[:, c0:c1]
        prev_last = jnp.where(first, 0.0, pp_ref[7:8, c0:c1])
        sh = jnp.where(row == 0, prev_last, pltpu.roll(p, 1, axis=0))
        return p + (sh - p) * mu_ref[:, c0:c1]

    C = RWKV_WIDTH
    r = mixed(0, C)
    k = mixed(C, 2 * C)
    v = mixed(2 * C, 3 * C)
    wa = mixed(3 * C, 3 * C + LANES)
    gl = mixed(3 * C + LANES, RWKV_IN_PAD)

    lora = lambda act, w_ref: _mm3(_split(act), (w_ref[0], w_ref[1]))
    wlog = w0_ref[...] + lora(jnp.tanh(wa), w2_ref)
    xs = -wlog
    softplus = jnp.maximum(xs, 0.0) + jnp.log(1.0 + jnp.exp(-jnp.abs(xs)))
    w = -softplus - 0.5
    log_decay = -jnp.exp(w)
    a = _sigmoid(a0_ref[...] + lora(wa, a2_ref))
    g = lora(_sigmoid(gl), g2_ref)
    if has_vres:
        v = v + (vf_ref[...] - v) * _sigmoid(v0_ref[...] + lora(lora(v, v1_ref), v2_ref))

    kk = k * kk_w_ref[...]
    nrm = jnp.sqrt(_segsum_heads(kk * kk, _head_ones().astype(BF16)))
    kk = kk / jnp.maximum(nrm, 1e-12)
    k = k * (1.0 + (a - 1.0) * ka_ref[...])

    ti = lax.broadcasted_iota(jnp.int32, (tm, tm), 0)
    si = lax.broadcasted_iota(jnp.int32, (tm, tm), 1)
    tri = jnp.where((si <= ti) & (si // CHUNK == ti // CHUNK), 1.0, 0.0).astype(BF16)
    cum = sum(jnp.dot(tri, part, preferred_element_type=F32) for part in _split3(log_decay))

    r_o[...] = r
    k_o[...] = k
    v_o[...] = v
    kk_o[...] = kk
    b_o[...] = kk * a
    cum_o[...] = cum
    g_o[...] = g


def rwkv_prep(p, mu, w0, a0, k_k, k_a, w2p, a2p, g2p, vres):
    S = p.shape[0]
    tm = TILES["rwkv_prep"]
    C = RWKV_WIDTH
    row = lambda i: (i, 0)
    vec = lambda i: (0, 0)
    whole = lambda w: pl.BlockSpec(w.shape, lambda i: (0,) * w.ndim)
    in_specs = [pl.BlockSpec((tm, RWKV_IN_PAD), row),
                pl.BlockSpec((8, RWKV_IN_PAD), lambda i: (jnp.maximum(i * (tm // 8) - 1, 0), 0)),
                pl.BlockSpec((1, RWKV_IN_PAD), vec),
                pl.BlockSpec((1, C), vec), pl.BlockSpec((1, C), vec), pl.BlockSpec((1, C), vec),
                pl.BlockSpec((1, C), vec), whole(w2p), whole(a2p), whole(g2p)]
    args = [p, p, mu, w0, a0, k_k, k_a, w2p, a2p, g2p]
    if vres is not None:
        v_first, v0, v1p, v2p = vres
        in_specs += [pl.BlockSpec((tm, C), row), pl.BlockSpec((1, C), vec), whole(v1p), whole(v2p)]
        args += [v_first, v0, v1p, v2p]
    out = jax.ShapeDtypeStruct((S, C), F32)
    return pl.pallas_call(
        functools.partial(_rwkv_prep_kernel, has_vres=vres is not None, tm=tm),
        out_shape=[out] * 7,
        grid=(S // tm,),
        in_specs=in_specs,
        out_specs=[pl.BlockSpec((tm, C), row)] * 7,
        compiler_params=_cparams(("parallel",)),
        name="rwkv_prep",
    )(*args)


_NN = (((1,), (0,)), ((), ()))
_NT = (((1,), (1,)), ((), ()))
_TN = (((0,), (0,)), ((), ()))


def _split(x):
    hi = x.astype(BF16)
    return hi, (x - hi.astype(F32)).astype(BF16)


def _mm3(a, b, dims=_NN):
    f = lambda x, y: lax.dot_general(x, y, dims, preferred_element_type=F32)
    return f(a[0], b[0]) + (f(a[0], b[1]) + f(a[1], b[0]))


def _split3(x):
    hi = x.astype(BF16)
    r1 = x - hi.astype(F32)
    mid = r1.astype(BF16)
    return hi, mid, (r1 - mid.astype(F32)).astype(BF16)


def _dot_ones(x, ones_bf16):
    hi, mid, lo = _split3(x)
    f = lambda t: jnp.dot(t, ones_bf16, preferred_element_type=F32)
    return f(hi) + (f(mid) + f(lo))


def _rwkv_chunk_kernel(r_ref, k_ref, v_ref, kk_ref, b_ref, cum_ref, g_ref, rk_ref, lnw_ref, lnb_ref,
                       o_ref, ht_ref, y_ref, *, n_chunks, n_pairs):
    L = CHUNK

    @pl.when(pl.program_id(1) == 0)
    def _():
        ht_ref[...] = jnp.zeros_like(ht_ref)

    row = lax.broadcasted_iota(jnp.int32, (L, LANES), 0)
    lane = lax.broadcasted_iota(jnp.int32, (L, LANES), 1)
    head0 = lane < HEAD_DIM
    ri = lax.broadcasted_iota(jnp.int32, (2 * L, 2 * L), 0)
    ci = lax.broadcasted_iota(jnp.int32, (2 * L, 2 * L), 1)
    same = (ri // L) == (ci // L)
    strict = same & ((ci % L) < (ri % L))
    incl = same & ((ci % L) <= (ri % L))

    def stack(x):
        return jnp.concatenate([jnp.where(head0, x, 0.0), jnp.where(head0, 0.0, x)], axis=0)

    items = [(c, p) for c in range(n_chunks) for p in range(n_pairs)]
    pre = [dict() for _ in items]
    for q, (c, p) in zip(pre, items):
        sl = (slice(c * L, (c + 1) * L), slice(p * LANES, (p + 1) * LANES))
        r, k, v, kk, b, cum = r_ref[sl], k_ref[sl], v_ref[sl], kk_ref[sl], b_ref[sl], cum_ref[sl]
        cum_excl = jnp.where(row == 0, 0.0, pltpu.roll(cum, 1, axis=0))
        e_pos = jnp.exp(cum)
        e_neg = jnp.exp(-cum)
        q["s_a"] = stack(-kk * jnp.exp(cum_excl))
        q["v"] = stack(v)
        q["s_r"] = stack(r * e_pos)
        sp_ar = _split(jnp.concatenate([q["s_a"], q["s_r"]], axis=0))
        q["bk"] = _split(jnp.concatenate([stack(b * e_neg), stack(k * e_neg)], axis=0))
        q["p_last"] = e_pos[L - 1:L, :]
        q["big"] = _mm3(sp_ar, q["bk"], _NT)
    for q in pre:
        big = q.pop("big")
        q["pw"] = jnp.where(strict, big[:2 * L, :2 * L], 0.0)
        q["a_ak"] = jnp.where(strict, big[:2 * L, 2 * L:], 0.0)
        q["rbk"] = _split(jnp.concatenate([jnp.where(incl, big[2 * L:, :2 * L], 0.0),
                                           jnp.where(incl, big[2 * L:, 2 * L:], 0.0)], axis=1))
    mm = lambda x, y: _mm3(_split(x), _split(y))
    mmb = lambda x, y: jnp.dot(x.astype(BF16), y.astype(BF16), preferred_element_type=F32)
    m0 = 8
    blk_of = lambda m: ((ri // m) == (ci // m))
    for q in pre:
        d1 = jnp.where(blk_of(m0), q["pw"], 0.0)
        q["d1"] = d1
        q["d2"] = mmb(d1, d1)
    for q in pre:
        d1, d2 = q.pop("d1"), q["d2"]
        q["n"] = d1 + d2 + mmb(d1, d2)
        q["d4"] = mmb(d2, d2)
        q.pop("d2")
    for q in pre:
        d4 = q.pop("d4")
        q["n"] = q["n"] + d4 + mmb(q["n"], d4)
    m = m0
    while m < L:
        for q in pre:
            e = jnp.where(blk_of(2 * m) & jnp.logical_not(blk_of(m)), q["pw"], 0.0)
            q["x"] = e + mmb(q["n"], e)
        for q in pre:
            x = q.pop("x")
            q["n"] = q["n"] + x + mmb(x, q["n"])
        m *= 2
    for q in pre:
        q.pop("pw")
        q["tmp"] = mm(q.pop("a_ak"), q["v"])
    for q in pre:
        x = jnp.concatenate([q.pop("s_a"), q.pop("tmp")], axis=1)
        q["wu"] = x + mm(q.pop("n"), x)
    for q in pre:
        wu = q.pop("wu")
        sp_w = _split(wu[:, :2 * L])
        sp_uv0 = _split(jnp.concatenate([wu[:, 2 * L:], q.pop("v")], axis=0))
        bk, rbk = q.pop("bk"), q.pop("rbk")
        q["mt"] = _split(_mm3(sp_w, (bk[0][:2 * L], bk[1][:2 * L]), _TN))
        q["gt"] = _mm3(sp_uv0, bk, _TN)
        q["rhat"] = _split(q.pop("s_r") + _mm3((rbk[0][:, :2 * L], rbk[1][:, :2 * L]), sp_w))
        q["yhat"] = _mm3(rbk, sp_uv0)

    hts = [ht_ref[p] for p in range(n_pairs)]
    for q, (c, p) in zip(pre, items):
        sp_h = _split(hts[p])
        y_s = _mm3(q["rhat"], sp_h, _NT) + q["yhat"]
        hts[p] = (hts[p] + _mm3(sp_h, q["mt"]) + q["gt"]) * q["p_last"]
        y_ref[c * L:(c + 1) * L, p * LANES:(p + 1) * LANES] = y_s[:L] + y_s[L:]
    for p in range(n_pairs):
        ht_ref[p] = hts[p]

    ones_bd = _head_ones().astype(BF16)
    inv_n = 1.0 / HEAD_DIM
    for p in range(n_pairs):
        cs = slice(p * LANES, (p + 1) * LANES)
        y = y_ref[:, cs]
        mean = _dot_ones(y, ones_bd) * inv_n
        yc = y - mean
        var = _dot_ones(yc * yc, ones_bd) * inv_n
        yn = yc * lax.rsqrt(var + GN_EPS) * lnw_ref[:, cs] + lnb_ref[:, cs]
        r, k, v = r_ref[:, cs], k_ref[:, cs], v_ref[:, cs]
        bonus = _dot_ones(r * k * rk_ref[:, cs], ones_bd)
        o_ref[:, cs] = ((yn + bonus * v) * g_ref[:, cs]).astype(o_ref.dtype)


def rwkv_chunk(r, k, v, kk, b, cum, g, r_k, ln_w, ln_b):
    S, C = r.shape
    n_chunks, n_pairs = TILES["rwkv_chunk"]
    tm = n_chunks * CHUNK
    tw = n_pairs * LANES
    blk = pl.BlockSpec((tm, tw), lambda h, c: (c, h))
    vec = pl.BlockSpec((1, tw), lambda h, c: (0, h))
    return pl.pallas_call(
        functools.partial(_rwkv_chunk_kernel, n_chunks=n_chunks, n_pairs=n_pairs),
        out_shape=jax.ShapeDtypeStruct((S, C), BF16),
        grid=(C // tw, S // tm),
        in_specs=[blk] * 7 + [vec] * 3,
        out_specs=blk,
        scratch_shapes=[pltpu.VMEM((n_pairs, LANES, LANES), F32), pltpu.VMEM((tm, tw), F32)],
        compiler_params=_cparams(("parallel", "arbitrary")),
        name="rwkv_chunk",
    )(r, k, v, kk, b, cum, g, r_k, ln_w, ln_b)


def _gelu_tanh(y):
    return 0.5 * y * (1.0 + jnp.tanh(math.sqrt(2.0 / math.pi) * (y + 0.044715 * (y * y * y))))


def _cmul(ar, ai, br, bi):
    return ar * br - ai * bi, ar * bi + ai * br


def _s5_kernel(u_ref, kall_ref, wz_ref, call_ref, pw_ref, ct_ref, d_ref, o_ref, carry_ref, ys_ref, *, tm, sub):
    half = 8 * S5_STATE
    Q = S5_LAG
    nq = tm // Q

    @pl.when(pl.program_id(1) == 0)
    def _():
        carry_ref[...] = jnp.zeros_like(carry_ref)

    zin = jnp.concatenate([u_ref[pl.ds(Q - 1 - tau, nq, stride=Q), :] for tau in range(Q)], axis=1)
    z = jnp.dot(zin.astype(BF16), wz_ref[0], preferred_element_type=F32)

    xr, xi = z[:, :half], z[:, half:]
    crow = lax.broadcasted_iota(jnp.int32, (nq, 1), 0)
    for lvl in range(int(math.log2(nq))):
        off = 1 << lvl
        ar, ai = pw_ref[0, lvl:lvl + 1, :half], pw_ref[0, lvl:lvl + 1, half:]
        keep = crow >= off
        sr = jnp.where(keep, pltpu.roll(xr, off, axis=0), 0.0)
        si = jnp.where(keep, pltpu.roll(xi, off, axis=0), 0.0)
        xr, xi = xr + ar * sr - ai * si, xi + ar * si + ai * sr
    cr, ci = carry_ref[:, :half], carry_ref[:, half:]
    tr, ti = _cmul(ct_ref[0, :, :half], ct_ref[0, :, half:], cr, ci)
    xr, xi = xr + tr, xi + ti
    er = jnp.where(crow == 0, cr, pltpu.roll(xr, 1, axis=0))
    ei = jnp.where(crow == 0, ci, pltpu.roll(xi, 1, axis=0))
    carry_ref[:, :half] = xr[nq - 1:nq, :]
    carry_ref[:, half:] = xi[nq - 1:nq, :]

    ysf = jnp.dot(jnp.concatenate([er, ei], axis=1).astype(BF16), call_ref[0],
                  preferred_element_type=F32)
    for s in range(Q):
        ys_ref[pl.ds(s, nq, stride=Q), :] = ysf[:, s * LANES:(s + 1) * LANES]

    pos = lax.broadcasted_iota(jnp.int32, (sub, 1), 0) % Q
    for t0 in range(0, tm, sub):
        u = u_ref[t0:t0 + sub, :]
        lags = [u] + [jnp.where(pos >= tau, pltpu.roll(u, tau, axis=0), 0.0) for tau in range(1, Q)]
        y = jnp.dot(jnp.concatenate(lags, axis=1).astype(BF16), kall_ref[0], preferred_element_type=F32)
        o_ref[t0:t0 + sub, :] = _gelu_tanh(y + ys_ref[t0:t0 + sub, :] + d_ref[...] * u)


def s5_scan(p, col0, tables, d):
    kall, wz, call, pows, ctab = tables
    tm, sub = TILES["s5"]
    S = p.shape[0]
    nj = S5_WIDTH // LANES
    nst = 2 * 8 * S5_STATE
    nq = tm // S5_LAG
    ql = S5_LAG * LANES
    per_j = lambda *shape: pl.BlockSpec((1,) + shape, lambda j, i: (j,) + (0,) * len(shape))
    return pl.pallas_call(
        functools.partial(_s5_kernel, tm=tm, sub=sub),
        out_shape=jax.ShapeDtypeStruct((S, S5_WIDTH), F32),
        grid=(nj, S // tm),
        in_specs=[pl.BlockSpec((tm, LANES), lambda j, i: (i, col0 // LANES + j)),
                  per_j(ql, LANES), per_j(ql, nst), per_j(nst, ql),
                  per_j(pows.shape[1], nst), per_j(nq, nst),
                  pl.BlockSpec((1, LANES), lambda j, i: (0, j))],
        out_specs=pl.BlockSpec((tm, LANES), lambda j, i: (i, j)),
        scratch_shapes=[pltpu.VMEM((1, nst), F32), pltpu.VMEM((tm, LANES), F32)],
        compiler_params=_cparams(("parallel", "arbitrary")),
        name="s5_scan",
    )(p, kall, wz, call, pows, ctab, d)


def _s5_tables(lam_re, lam_im, log_step, b_re, b_im, c_re, c_im):
    G, P, GS = S5_WIDTH // S5_GROUP, S5_STATE, S5_GROUP
    nj = G // 8
    Q = S5_LAG
    nq = TILES["s5"][0] // Q
    lam = lax.complex(jnp.minimum(lam_re, -1e-4), lam_im)
    step = jnp.exp(log_step)[:, None]
    lam_dt = lam * step
    lam_bar = jnp.exp(lam_dt)
    b_bar = ((lam_bar - 1.0) / lam)[..., None] * lax.complex(b_re, b_im)

    def lam_pow(e):
        return jnp.exp(lam_dt[None] * e.astype(F32)[:, None, None])

    def pack(z):
        z = jnp.moveaxis(z.reshape(z.shape[0], nj, 8 * P), 1, 0)
        return jnp.concatenate([jnp.real(z), jnp.imag(z)], axis=-1).astype(F32)

    pows = pack(lam_pow(Q * 2 ** jnp.arange(int(math.log2(nq)))))
    ctab = pack(lam_pow(Q * (jnp.arange(nq) + 1)))

    eye8 = jnp.eye(8, dtype=F32)
    ql = Q * 8 * GS
    grp_of_ql = (jnp.arange(ql) // GS) % 8
    grp_of_state = jnp.arange(8 * P) // P
    m = lam_pow(jnp.arange(Q))[..., None] * b_bar[None]
    def bd_in(x):
        rows = jnp.transpose(x.reshape(Q, nj, 8, P, GS), (1, 0, 2, 4, 3)).reshape(nj, ql, P)
        return jnp.where(grp_of_ql[:, None] == grp_of_state[None, :], jnp.tile(rows, (1, 1, 8)), 0.0)
    wz = jnp.concatenate([bd_in(jnp.real(m)), bd_in(jnp.imag(m))], axis=-1).astype(F32)
    ls = lam_pow(jnp.arange(Q) + 1)[:, :, None, :]
    lr, li = jnp.real(ls), jnp.imag(ls)
    def bd_out(x):
        cols = jnp.transpose(x.reshape(Q, nj, 8, GS, P), (1, 4, 0, 2, 3)).reshape(nj, P, ql)
        return jnp.where(grp_of_state[:, None] == grp_of_ql[None, :], jnp.tile(cols, (1, 8, 1)), 0.0)
    call = jnp.concatenate([bd_out(c_re[None] * lr - c_im[None] * li),
                            -bd_out(c_re[None] * li + c_im[None] * lr)], axis=1).astype(F32)
    ktab = (jnp.einsum('ghp,tgpk->tghk', c_re, jnp.real(m), precision=HI)
            - jnp.einsum('ghp,tgpk->tghk', c_im, jnp.imag(m), precision=HI))
    kall = jnp.einsum('tjghk,gm->jtgkmh', ktab.reshape(Q, nj, 8, GS, GS), eye8).reshape(nj, Q * 8 * GS, 8 * GS)
    return kall.astype(BF16), wz.astype(BF16), call.astype(BF16), pows, ctab


def _glu_kernel(y_ref, yc_ref, w_ref, b_ref, o_ref, a_ref):
    @pl.when(pl.program_id(1) == 0)
    def _():
        a_ref[...] = y_ref[...].astype(BF16)

    z = _dot(a_ref[...], w_ref[...]) + b_ref[...]
    o_ref[...] = (yc_ref[...] * _sigmoid(z)).astype(o_ref.dtype)


def glu(y, w, layer, b):
    S, C = y.shape
    tm, tn = TILES["glu"]
    return pl.pallas_call(
        _glu_kernel,
        out_shape=jax.ShapeDtypeStruct((S, C), BF16),
        grid=(S // tm, C // tn),
        in_specs=[pl.BlockSpec((tm, C), lambda i, j: (i, 0)), pl.BlockSpec((tm, tn), lambda i, j: (i, j)),
                  pl.BlockSpec((None, C, tn), lambda i, j: (layer, 0, j)),
                  pl.BlockSpec((1, tn), lambda i, j: (0, j))],
        out_specs=pl.BlockSpec((tm, tn), lambda i, j: (i, j)),
        scratch_shapes=[pltpu.VMEM((tm, C), BF16)],
        compiler_params=_cparams(("parallel", "arbitrary")),
        name="s5_glu",
    )(y, y, w, b)


def _attn_block(qs, ks, vs, bias, head0):
    zero = jnp.zeros((), qs.dtype)
    hb = qs.shape[0] // 2
    head0 = lax.broadcasted_iota(jnp.int32, (hb, LANES), 1) < HEAD_DIM
    outs, lse_out = [], []
    for t in range(2):
        rows = slice(t * hb, (t + 1) * hb)
        pvs, dens, lses = [], [], []
        for hmask in (head0, jnp.logical_not(head0)):
            s = _dot_nt(jnp.where(hmask, qs[rows], zero), ks) + bias[rows]
            m = jnp.max(s, axis=-1, keepdims=True)
            pexp = jnp.exp(s - m)
            den = jnp.sum(pexp, axis=-1, keepdims=True)
            pvs.append(_dot(pexp.astype(vs.dtype), vs))
            dens.append(den)
            lses.append(m + jnp.log(den))
        outs.append(jnp.where(head0, pvs[0], pvs[1]) / jnp.where(head0, dens[0], dens[1]))
        lse_out.append(jnp.where(head0, lses[0], lses[1]))
    return jnp.concatenate(outs, axis=0), jnp.concatenate(lse_out, axis=0)


def _attn_kernel(*refs, dilations, tile):
    nb = len(dilations)
    in_refs = [refs[5 * b:5 * b + 5] for b in range(nb)]
    out_ref, o_s, l_s = refs[5 * nb:]
    B = ATTN_BLK
    first_tile = pl.program_id(1) == 0
    head0 = lax.broadcasted_iota(jnp.int32, (B, LANES), 1) < HEAD_DIM
    qi = lax.broadcasted_iota(jnp.int32, (B, 2 * B), 0)
    kj = lax.broadcasted_iota(jnp.int32, (B, 2 * B), 1)
    in_band = (kj >= qi) & (kj <= qi + B)
    band = jnp.where(in_band, 0.0, -jnp.inf)
    band_first = jnp.where(in_band & (kj >= jnp.where(first_tile, B, 0)), 0.0, -jnp.inf)
    scale = HEAD_DIM ** -0.5

    for b, d in enumerate(dilations):
        q_ref, kc_ref, kp_ref, vc_ref, vp_ref = in_refs[b]
        n_blk = tile // d // B

        def block(r, row0, first_blk, b=b, d=d, refs_=in_refs[b]):
            q_ref, kc_ref, kp_ref, vc_ref, vp_ref = refs_
            qs = q_ref[r, pl.ds(row0, B), :] * scale
            if first_blk:
                ks = jnp.concatenate([kp_ref[r], kc_ref[r, :B, :]], axis=0)
                vs = jnp.concatenate([vp_ref[r], vc_ref[r, :B, :]], axis=0)
            else:
                ks = kc_ref[r, pl.ds(row0 - B, 2 * B), :]
                vs = vc_ref[r, pl.ds(row0 - B, 2 * B), :]
            o, lse = _attn_block(qs, ks, vs, band_first if first_blk else band, head0)
            dst = pl.ds(row0 * d + r, B, stride=d) if d > 1 else pl.ds(row0, B)
            o_s[b, dst, :] = o
            l_s[b, dst, :] = lse

        for r in range(d):
            for qb in range(n_blk):
                block(r, qb * B, qb == 0)

    rc = 2 * B

    def merge(c, carry):
        sl = pl.ds(pl.multiple_of(c * rc, rc), rc)
        ls_ = [l_s[b, sl, :] for b in range(nb)]
        m = functools.reduce(jnp.maximum, ls_)
        es = [jnp.exp(l - m) for l in ls_]
        num = functools.reduce(lambda x, y: x + y, [e * o_s[b, sl, :] for b, e in enumerate(es)])
        den = functools.reduce(lambda x, y: x + y, es)
        out_ref[sl, :] = (num / den).astype(out_ref.dtype)
        return carry
    lax.fori_loop(0, tile // rc, merge, 0)


def dilated_attention(qkvs, tile=ATTN_TILE):
    dilations = tuple(t.shape[0] for t in qkvs)
    S = qkvs[0].shape[0] * qkvs[0].shape[1]
    nh = D_MODEL // LANES
    in_specs, args = [], []
    for t, d in zip(qkvs, dilations):
        rows = tile // d
        rpb = rows // ATTN_BLK

        def cur(part, rows=rows, d=d):
            return pl.BlockSpec((d, rows, LANES), lambda h, i: (0, i, part * nh + h))

        def prev(part, rpb=rpb, d=d):
            return pl.BlockSpec((d, ATTN_BLK, LANES),
                                lambda h, i: (0, jnp.maximum(i * rpb - 1, 0), part * nh + h))

        in_specs += [cur(0), cur(1), prev(1), cur(2), prev(2)]
        args += [t] * 5
    nb = len(dilations)
    return pl.pallas_call(
        functools.partial(_attn_kernel, dilations=dilations, tile=tile),
        out_shape=jax.ShapeDtypeStruct((S, D_MODEL), BF16),
        grid=(nh, S // tile),
        in_specs=in_specs,
        out_specs=pl.BlockSpec((tile, LANES), lambda h, i: (i, h)),
        scratch_shapes=[pltpu.VMEM((nb, tile, LANES), F32), pltpu.VMEM((nb, tile, LANES), F32)],
        compiler_params=_cparams(("parallel", "arbitrary")),
        name="dilated_attention",
    )(*args)


def _rotary_tables(S):
    half = ROT_DIM // 2
    inv = ROPE_THETA ** (-jnp.arange(half, dtype=F32) * 2.0 / ROT_DIM)
    ang = jnp.arange(S, dtype=F32)[:, None] * inv[None, :]
    cos, sin = jnp.cos(ang), jnp.sin(ang)
    rest = HEAD_DIM - ROT_DIM
    ones = jnp.ones((S, rest), F32)
    zeros = jnp.zeros((S, rest), F32)
    zh = jnp.zeros((S, half), F32)
    cos_t = jnp.concatenate([cos, cos, ones], axis=1)
    sin_a = jnp.concatenate([-sin, zh, zeros], axis=1)
    sin_b = jnp.concatenate([zh, sin, zeros], axis=1)
    rep = LANES // HEAD_DIM
    return tuple(jnp.tile(t, (1, rep)) for t in (cos_t, sin_a, sin_b))


def _pad_rows(w, n):
    return jnp.pad(w, ((0, n - w.shape[0]), (0, 0)))


def _split_w(w):
    hi = w.astype(BF16)
    return jnp.stack([hi, (w - hi.astype(F32)).astype(BF16)])


def kernel(x, c, ada_w, ada_b, norm_mix_g, norm_ffn_g, hyb_w_in, hyb_w_out, rwkv_mu, rwkv_w0, rwkv_w2, rwkv_a0, rwkv_a2, rwkv_g2, rwkv_k_k, rwkv_k_a, rwkv_r_k, rwkv_ln_w, rwkv_ln_b, rwkv_v0, rwkv_v1, rwkv_v2, s5_lam_re, s5_lam_im, s5_log_step, s5_b_re, s5_b_im, s5_c_re, s5_c_im, s5_d, s5_glu_w, s5_glu_b, attn_w_qkv, attn_w_o, ffn_w_in, ffn_w_out, final_norm_g):
    B, S, D = x.shape
    assert B == 1 and D == D_MODEL and S % 2048 == 0
    C = RWKV_WIDTH
    xs = x.reshape(S, D)
    mod = ada_modulation(c, ada_w, ada_b)
    rot_tables = _rotary_tables(S)
    dilations = tuple(d for (_, d) in DILATED_BRANCHES)
    row = lambda t: t.reshape(1, -1)
    mods = [[mod[i][:, q * D:(q + 1) * D] for q in range(6)] for i in range(DEPTH)]
    mix_norm = lambda i: (row(norm_mix_g[i]), mods[i][0], mods[i][1])
    ffn_norm = lambda i: (row(norm_ffn_g[i]), mods[i][3], mods[i][4])

    gpad = RWKV_IN_PAD - RWKV_IN
    n_even = hyb_w_in.shape[0]
    w_hyb_in = jnp.concatenate([hyb_w_in[:, :, :RWKV_IN], jnp.zeros((n_even, D, gpad), F32), hyb_w_in[:, :, RWKV_IN:],
                                jnp.zeros((n_even, D, EVEN_IN_TILED - EVEN_IN_PAD), F32)], axis=2).astype(BF16)
    w_hyb_out, w_glu = hyb_w_out.astype(BF16), s5_glu_w.astype(BF16)
    w_qkv, w_attn_o = attn_w_qkv.astype(BF16), attn_w_o.astype(BF16)
    w_ffn_in, w_ffn_out = ffn_w_in.astype(BF16), ffn_w_out.astype(BF16)

    v_first = None
    h = norm_mod_rows(xs, mix_norm(0))
    for i in range(DEPTH):
        gt_mix, gt_ffn = mods[i][2], mods[i][5]
        j = i // 2
        if i % 2 == 0:
            mu = jnp.concatenate([rwkv_mu[j], jnp.zeros((gpad,), F32)]).reshape(1, RWKV_IN_PAD)
            p = in_proj(h, w_hyb_in, j)

            w2p = _split_w(jnp.concatenate([rwkv_w2[j], jnp.zeros((LORA_A, C), F32)], axis=0))
            a2p = _split_w(jnp.concatenate([jnp.zeros((LORA_W, C), F32), rwkv_a2[j]], axis=0))
            g2p = _split_w(_pad_rows(rwkv_g2[j], 2 * LANES))
            vres = None
            if j > 0:
                v1p = _split_w(jnp.pad(rwkv_v1[j - 1], ((0, 0), (0, LANES - LORA_V))))
                v2p = _split_w(_pad_rows(rwkv_v2[j - 1], LANES))
                vres = (v_first, row(rwkv_v0[j - 1]), v1p, v2p)
            r, k, v, kk, b, cum, g = rwkv_prep(p, mu, row(rwkv_w0[j]), row(rwkv_a0[j]), row(rwkv_k_k[j]),
                                               row(rwkv_k_a[j]), w2p, a2p, g2p, vres)
            if j == 0:
                v_first = v
            y_rwkv = rwkv_chunk(r, k, v, kk, b, cum, g, row(rwkv_r_k[j]), row(rwkv_ln_w[j]),
                                row(rwkv_ln_b[j]))

            s5_tabs = _s5_tables(s5_lam_re[j], s5_lam_im[j], s5_log_step[j], s5_b_re[j], s5_b_im[j],
                                 s5_c_re[j], s5_c_im[j])
            y_s5 = s5_scan(p, RWKV_IN_PAD, s5_tabs, row(s5_d[j]))
            y_s5 = glu(y_s5, w_glu, j, row(s5_glu_b[j]))
            xs, h = proj_residual([y_rwkv, y_s5], w_hyb_out, j, xs, gt_mix, ffn_norm(i))
        else:
            qkvs = qkv_proj(h, w_qkv, j, rot_tables, 2 * D, dilations)
            o = dilated_attention(qkvs)
            xs, h = proj_residual([o], w_attn_o, j, xs, gt_mix, ffn_norm(i))
        if i < DEPTH - 1:
            xs, h = ffn(h, xs, gt_ffn, w_ffn_in, w_ffn_out, i, mix_norm(i + 1), last=False)
        else:
            zero = jnp.zeros((1, D), F32)
            (xs,) = ffn(h, xs, gt_ffn, w_ffn_in, w_ffn_out, i, (row(final_norm_g), zero, zero), last=True)
    return xs.reshape(B, S, D)
```
